```python
import math
import jax
import jax.numpy as jnp
from jax import lax
import numpy as np

D_MODEL = 1024
BATCH = 16
SEQ = 4096
DEPTH = 2

CTX_LEN = 256
GRID_W = 64
EPS = 1e-6

D_CHUNK = D_MODEL // 2
CHUNK = 128
A_GROUPS = 4
A_GROUP_W = D_CHUNK // A_GROUPS

D_SSM = D_MODEL
SSM_HEAD_DIM = 64
SSM_HEADS = D_SSM // SSM_HEAD_DIM
SSM_GROUPS = 2
SSM_STATE = 128
SSM_CONV = 5
SSM_CHUNK = 128

D_ATT = D_MODEL // 2
ATT_QK_DIM = 64
ATT_V_DIM = 2 * ATT_QK_DIM
ATT_HEADS = D_ATT // ATT_V_DIM
D_QK = ATT_HEADS * 2 * ATT_QK_DIM
ROPE_THETA = 10000.0
Q_BLOCK = 128

D_MIX = D_CHUNK + D_SSM + D_ATT
N_CONV_CH = D_SSM + 2 * SSM_GROUPS * SSM_STATE
N_DT = 2 * SSM_HEADS
CTX_SIZES = (N_CONV_CH, N_DT, D_QK, D_ATT)
LAT_SIZES = (D_CHUNK, D_CHUNK, D_CHUNK, D_SSM, D_QK, D_ATT)
N_CTX_COLS = N_CONV_CH + N_DT + D_QK + D_ATT
D_IN_PROJ = N_CTX_COLS + 3 * D_CHUNK + D_SSM + D_QK + D_ATT

kernel_name = "hybrid_diffusion_trunk"


def _cuts(sizes):
    cuts, acc = [], 0
    for s in sizes[:-1]:
        acc += s
        cuts.append(acc)
    return cuts


def rms_norm(x, gain=None):
    xf = x.astype(jnp.float32)
    y = xf * lax.rsqrt(jnp.mean(xf * xf, axis=-1, keepdims=True) + EPS)
    if gain is not None:
        y = y * gain.astype(jnp.float32)
    return y.astype(x.dtype)


def axial_rope(n_tokens):
    rows = n_tokens // GRID_W
    row = jnp.repeat(jnp.arange(rows), GRID_W).astype(jnp.float32)
    col = jnp.tile(jnp.arange(GRID_W), rows).astype(jnp.float32)
    n_freq = ATT_QK_DIM // 4
    inv = ROPE_THETA ** (-jnp.arange(n_freq, dtype=jnp.float32) / n_freq)
    ang = jnp.concatenate([row[:, None] * inv, col[:, None] * inv], axis=-1)
    return jnp.cos(ang), jnp.sin(ang)


def apply_rope(t, cos, sin):
    half = t.shape[-1] // 2
    c = cos[None, :, None, None, :].astype(t.dtype)
    s = sin[None, :, None, None, :].astype(t.dtype)
    t1, t2 = t[..., :half], t[..., half:]
    return jnp.concatenate([t1 * c - t2 * s, t1 * s + t2 * c], axis=-1)


def chunk_mlp(u, v, g, norm_g, ws, bs):
    b, n, _ = u.shape
    nc = n // CHUNK
    u = jax.nn.gelu(u)
    v = rms_norm(jax.nn.gelu(v), norm_g)
    vr = v.reshape(b, nc, CHUNK, A_GROUPS, A_GROUP_W)
    mixed = jnp.einsum('gts,bcsgw->bctgw', ws.astype(vr.dtype), vr) + bs.T[None, None, :, :, None].astype(vr.dtype)
    y = u * mixed.reshape(b, n, D_CHUNK)
    return y * jax.nn.silu(g)


def dwconv_centred(x, w, bias):
    ch = x.shape[-1]
    k = w.shape[0]
    y = lax.conv_general_dilated(x, w[:, None, :].astype(x.dtype), window_strides=(1,),
                                 padding=[(k // 2, k // 2)], dimension_numbers=('NWC', 'WIO', 'NWC'),
                                 feature_group_count=ch)
    return y + bias.astype(x.dtype)


def segsum_exp(a):
    L = a.shape[-1]
    cs = jnp.cumsum(a, axis=-1)
    diff = cs[..., :, None] - cs[..., None, :]
    mask = jnp.tril(jnp.ones((L, L), dtype=bool))
    return jnp.where(mask, jnp.exp(jnp.where(mask, diff, 0.0)), 0.0)


def ssd_prepare(xbc, dt_raw, conv_w, conv_b, dt_bias):
    xbc = jax.nn.silu(dwconv_centred(xbc, conv_w, conv_b))
    xs, bm, cm = jnp.split(xbc, [D_SSM, D_SSM + SSM_GROUPS * SSM_STATE], axis=-1)
    b, n, _ = xs.shape
    xh = xs.reshape(b, n, SSM_HEADS, SSM_HEAD_DIM)
    bm = bm.reshape(b, n, SSM_GROUPS, SSM_STATE)
    cm = cm.reshape(b, n, SSM_GROUPS, SSM_STATE)
    dt = jax.nn.softplus(dt_raw.reshape(b, n, 2, SSM_HEADS).astype(jnp.float32) + dt_bias.astype(jnp.float32))
    return xh, bm, cm, dt


def ssd_scan(xh, dt, a_neg, bm, cm, h0, with_output):
    b, n, H, P = xh.shape
    G, S = bm.shape[2], bm.shape[3]
    R = H // G
    L = SSM_CHUNK
    nc = n // L
    xdt = (xh.astype(jnp.float32) * dt[..., None]).reshape(b, nc, L, G, R, P)
    a = jnp.moveaxis((dt * a_neg).reshape(b, nc, L, G, R), 2, -1)
    a_cs = jnp.cumsum(a, axis=-1)
    bc = bm.astype(jnp.float32).reshape(b, nc, L, G, S)
    cc = cm.astype(jnp.float32).reshape(b, nc, L, G, S)
    decay_to_end = jnp.exp(a_cs[..., -1:] - a_cs)
    states = jnp.einsum('bclgs,bcgrl,bclgrp->bcgrps', bc, decay_to_end, xdt)
    states = jnp.concatenate([h0.astype(jnp.float32).reshape(b, 1, G, R, P, S), states], axis=1)
    chunk_tot = jnp.pad(jnp.moveaxis(a_cs[..., -1], 1, -1), ((0, 0), (0, 0), (0, 0), (1, 0)))
    chunk_decay = segsum_exp(chunk_tot)
    states = jnp.einsum('bgrzc,bcgrps->bzgrps', chunk_decay, states)
    h_final = states[:, -1].reshape(b, H, P, S)
    if not with_output:
        return None, h_final
    lmat = segsum_exp(a)
    cb = jnp.einsum('bclgs,bcmgs->bcglm', cc, bc)
    y_diag = jnp.einsum('bcglm,bcgrlm,bcmgrp->bclgrp', cb, lmat, xdt)
    y_off = jnp.einsum('bclgs,bcgrps,bcgrl->bclgrp', cc, states[:, :-1], jnp.exp(a_cs))
    return (y_diag + y_off).reshape(b, n, H, P), h_final


def ssd_bidir(xh, bm, cm, dt, a_neg, h0_f, h0_b, with_output):
    rev = lambda t: jnp.flip(t, axis=1)
    y_f, h_f = ssd_scan(xh, dt[:, :, 0], a_neg[0], bm, cm, h0_f, with_output)
    y_b, h_b = ssd_scan(rev(xh), rev(dt[:, :, 1]), a_neg[1], rev(bm), rev(cm), h0_b, with_output)
    y = y_f + rev(y_b) if with_output else None
    return y, h_f, h_b


def ssd_output(y, xh, z, d_skip, norm_g):
    b, n = z.shape[:2]
    y = (y + d_skip[:, None].astype(jnp.float32) * xh.astype(jnp.float32)).reshape(b, n, D_SSM)
    y = y * jax.nn.silu(z.astype(jnp.float32))
    y = rms_norm(y.reshape(b, n, SSM_GROUPS, D_SSM // SSM_GROUPS)).reshape(b, n, D_SSM) * norm_g.astype(jnp.float32)
    return y.astype(z.dtype)


def diff_lambda(lam_p, layer_idx):
    lam_init = 0.8 - 0.6 * math.exp(-0.3 * layer_idx)
    lp = lam_p.astype(jnp.float32)
    lam = jnp.exp(jnp.sum(lp[0] * lp[1])) - jnp.exp(jnp.sum(lp[2] * lp[3])) + lam_init
    return lam, lam_init


def diff_attend(q, k, v, lam):
    s = jnp.einsum('bqhid,bkhid->bhiqk', q, k).astype(jnp.float32) * (ATT_QK_DIM ** -0.5)
    p = jax.nn.softmax(s, axis=-1)
    w = p[:, :, 0] - lam * p[:, :, 1]
    return jnp.einsum('bhqk,bkhe->bqhe', w.astype(v.dtype), v)


def latent_diff_attention(q, k, v, k_ctx, v_ctx, lam):
    b, n = q.shape[:2]
    k_all = jnp.concatenate([k, k_ctx.astype(k.dtype)], axis=1)
    v_all = jnp.concatenate([v, v_ctx.astype(v.dtype)], axis=1)
    qb = q.reshape(b, n // Q_BLOCK, Q_BLOCK, ATT_HEADS, 2, ATT_QK_DIM).swapaxes(0, 1)
    ob = lax.map(lambda qq: diff_attend(qq, k_all, v_all, lam), qb)
    return ob.swapaxes(0, 1).reshape(b, n, ATT_HEADS, ATT_V_DIM)


def diff_post(o, subln_g, lam_init, g):
    b, n = o.shape[:2]
    o = rms_norm(o, subln_g) * (1.0 - lam_init)
    return o.reshape(b, n, D_ATT) * jax.nn.silu(g)


def layer_forward(x, xc, cos, sin, c, c_ctx, w_ada, b_ada, w_in, w_out, chunk_norm_g, chunk_ws,
                  chunk_bs, conv_w, conv_b, dt_bias, a_log, d_skip, ssm_norm_g, lam_p, subln_g,
                  layer_idx, update_ctx):
    b, n, _ = x.shape
    m = xc.shape[1]
    shift, scale, gate = jnp.split(jax.nn.silu(c) @ w_ada + b_ada, 3, axis=-1)
    shift_c, scale_c, gate_c = jnp.split(jax.nn.silu(c_ctx) @ w_ada + b_ada, 3, axis=-1)
    h = rms_norm(x) * (1.0 + scale[:, None]) + shift[:, None]
    hc = rms_norm(xc) * (1.0 + scale_c) + shift_c
    lam, lam_init = diff_lambda(lam_p, layer_idx)
    a_neg = -jnp.exp(a_log.astype(jnp.float32))

    pc = hc @ (w_in if update_ctx else w_in[:, :N_CTX_COLS])
    xbc_c, dt_c, k_c, v_c = jnp.split(pc[..., :N_CTX_COLS], _cuts(CTX_SIZES), axis=-1)
    xh_c, bm_c, cm_c, dts_c = ssd_prepare(xbc_c, dt_c, conv_w, conv_b, dt_bias)
    h0 = jnp.zeros((b, SSM_HEADS, SSM_HEAD_DIM, SSM_STATE), jnp.float32)
    ys_c, hf_c, hb_c = ssd_bidir(xh_c, bm_c, cm_c, dts_c, a_neg, h0, h0, update_ctx)
    k_c = k_c.reshape(b, m, ATT_HEADS, 2, ATT_QK_DIM)
    v_c = v_c.reshape(b, m, ATT_HEADS, ATT_V_DIM)
    if update_ctx:
        u_c, uv_c, ga_c, z_c, q_c, gc_c = jnp.split(pc[..., N_CTX_COLS:], _cuts(LAT_SIZES), axis=-1)
        ya_c = chunk_mlp(u_c, uv_c, ga_c, chunk_norm_g, chunk_ws, chunk_bs)
        yb_c = ssd_output(ys_c, xh_c, z_c, d_skip, ssm_norm_g)
        o_c = diff_attend(q_c.reshape(b, m, ATT_HEADS, 2, ATT_QK_DIM), k_c, v_c, lam)
        yc_c = diff_post(o_c, subln_g, lam_init, gc_c)
        xc = xc + gate_c * (jnp.concatenate([ya_c, yb_c, yc_c], axis=-1) @ w_out)
    else:
        xc = None

    p = h @ w_in
    xbc, dt_raw, k, v, u, uv, g_a, z, q, g_c = jnp.split(p, _cuts(CTX_SIZES + LAT_SIZES), axis=-1)
    ya = chunk_mlp(u, uv, g_a, chunk_norm_g, chunk_ws, chunk_bs)
    xh, bm, cm, dts = ssd_prepare(xbc, dt_raw, conv_w, conv_b, dt_bias)
    ys, _, _ = ssd_bidir(xh, bm, cm, dts, a_neg, hf_c, hb_c, True)
    yb = ssd_output(ys, xh, z, d_skip, ssm_norm_g)
    q = apply_rope(q.reshape(b, n, ATT_HEADS, 2, ATT_QK_DIM), cos, sin)
    k = apply_rope(k.reshape(b, n, ATT_HEADS, 2, ATT_QK_DIM), cos, sin)
    o = latent_diff_attention(q, k, v.reshape(b, n, ATT_HEADS, ATT_V_DIM), k_c, v_c, lam)
    yc = diff_post(o, subln_g, lam_init, g_c)
    x = x + gate[:, None] * (jnp.concatenate([ya, yb, yc], axis=-1) @ w_out)
    return x, xc


def setup_inputs(seed: int = 0) -> dict:
    key = jax.random.key(seed)
    ks = jax.random.split(key, 20)
    f32 = jnp.float32

    def nrm(k, shape, scale):
        return jax.random.normal(k, shape, f32) * scale

    dt0 = jnp.exp(jax.random.uniform(ks[13], (DEPTH, 2, SSM_HEADS), f32, math.log(1e-3), math.log(1e-1)))
    return {
        "x": nrm(ks[0], (BATCH, SEQ, D_MODEL), 1.0),
        "c": nrm(ks[1], (BATCH, D_MODEL), 1.0),
        "ctx": nrm(ks[2], (BATCH, CTX_LEN, D_MODEL), 1.0),
        "c_ctx": nrm(ks[3], (D_MODEL,), 1.0),
        "w_ada": nrm(ks[4], (DEPTH, D_MODEL, 3 * D_MODEL), 0.5 * D_MODEL ** -0.5),
        "b_ada": nrm(ks[5], (DEPTH, 3 * D_MODEL), 0.01),
        "w_in": nrm(ks[6], (DEPTH, D_MODEL, D_IN_PROJ), D_MODEL ** -0.5),
        "w_out": nrm(ks[7], (DEPTH, D_MIX, D_MODEL), D_MIX ** -0.5),
        "chunk_norm_g": 1.0 + nrm(ks[8], (DEPTH, D_CHUNK), 0.01),
        "chunk_ws": nrm(ks[9], (DEPTH, A_GROUPS, CHUNK, CHUNK), CHUNK ** -0.5),
        "chunk_bs": 1.0 + nrm(ks[10], (DEPTH, A_GROUPS, CHUNK), 0.01),
        "ssm_conv_w": nrm(ks[11], (DEPTH, SSM_CONV, N_CONV_CH), SSM_CONV ** -0.5),
        "ssm_conv_b": nrm(ks[12], (DEPTH, N_CONV_CH), 0.01),
        "ssm_dt_bias": dt0 + jnp.log(-jnp.expm1(-dt0)),
        "ssm_a_log": jnp.log(jax.random.uniform(ks[14], (DEPTH, 2, SSM_HEADS), f32, 1.0, 16.0)),
        "ssm_d": 1.0 + nrm(ks[15], (DEPTH, SSM_HEADS), 0.01),
        "ssm_norm_g": 1.0 + nrm(ks[16], (DEPTH, D_SSM), 0.01),
        "diff_lambda_p": nrm(ks[17], (DEPTH, 4, ATT_QK_DIM), 0.1),
        "diff_subln_g": 1.0 + nrm(ks[18], (DEPTH, ATT_V_DIM), 0.01),
        "final_norm_g": 1.0 + nrm(ks[19], (D_MODEL,), 0.01),
    }


def reference(x, c, ctx, c_ctx, w_ada, b_ada, w_in, w_out, chunk_norm_g, chunk_ws, chunk_bs,
              ssm_conv_w, ssm_conv_b, ssm_dt_bias, ssm_a_log, ssm_d, ssm_norm_g,
              diff_lambda_p, diff_subln_g, final_norm_g):
    cos, sin = axial_rope(x.shape[1])
    xc = ctx
    for li in range(DEPTH):
        x, xc = layer_forward(x, xc, cos, sin, c, c_ctx, w_ada[li], b_ada[li], w_in[li], w_out[li],
                              chunk_norm_g[li], chunk_ws[li], chunk_bs[li], ssm_conv_w[li], ssm_conv_b[li],
                              ssm_dt_bias[li], ssm_a_log[li], ssm_d[li], ssm_norm_g[li],
                              diff_lambda_p[li], diff_subln_g[li], li, li < DEPTH - 1)
    return rms_norm(x, final_norm_g)
```

```python
import functools
import math

import jax
import jax.numpy as jnp
from jax import lax
from jax.experimental import pallas as pl
from jax.experimental.pallas import tpu as pltpu

F32 = jnp.float32
BF16 = jnp.bfloat16

EPS = 1e-6
LANES = 128
SUBLANES = 8
V7X_VMEM_BYTES = 64 * 1024 * 1024
VMEM_LIMIT = V7X_VMEM_BYTES * 3 // 4

D_MODEL = 1024
D_CHUNK = 512
CHUNK = 128
A_GROUPS = 4
D_SSM = 1024
SSM_HEADS = 16
SSM_HEAD_DIM = 64
SSM_GROUPS = 2
SSM_STATE = 128
SSM_CONV = 5
SSM_CHUNK = 128
GROUP_W = D_SSM // SSM_GROUPS
HEADS_PER_GROUP = SSM_HEADS // SSM_GROUPS
D_ATT = 512
ATT_QK_DIM = 64
ATT_V_DIM = 128
ATT_HEADS = 4
D_QK = 512
ROPE_THETA = 10000.0
GRID_W = 64
N_CONV_CH = D_SSM + 2 * SSM_GROUPS * SSM_STATE
N_DT = 2 * SSM_HEADS
D_MIX = D_CHUNK + D_SSM + D_ATT

_PACK_SIZES = (("xbc", N_CONV_CH), ("dt", LANES), ("k", D_QK), ("v", D_ATT), ("u", D_CHUNK), ("vm", D_CHUNK),
               ("ga", D_CHUNK), ("z", D_SSM), ("q", D_QK), ("gc", D_ATT))
_PACK = {}
_off = 0
for _name, _size in _PACK_SIZES:
    _PACK[_name] = (_off, _off + _size)
    _off += _size
PACK_COLS = _off
PACK_CTX_COLS = _PACK["v"][1]

TM_IN = 256
TM_OUT = 512
TQ = 256
TK = 512


def _cparams(semantics):
    return pltpu.CompilerParams(dimension_semantics=semantics, vmem_limit_bytes=VMEM_LIMIT)


def _silu(t):
    return t * jax.nn.sigmoid(t)


def _split3(a):
    hi = a.astype(BF16)
    r = a - hi.astype(F32)
    mid = r.astype(BF16)
    lo = (r - mid.astype(F32)).astype(BF16)
    return hi, mid, lo


def _dot(a, b):
    return jnp.dot(a, b, preferred_element_type=F32)


def _dot_nt(a, b):
    return lax.dot_general(a, b, (((1,), (1,)), ((), ())), preferred_element_type=F32)


def _ada_kernel(c_ref, w_ref, b_ref, o_ref):
    s = _silu(c_ref[...])
    w = w_ref[...]
    s_hi = s.astype(BF16)
    s_lo = (s - s_hi.astype(F32)).astype(BF16)
    w_hi = w.astype(BF16)
    w_lo = (w - w_hi.astype(F32)).astype(BF16)
    o_ref[...] = _dot(s_hi, w_hi) + (_dot(s_hi, w_lo) + _dot(s_lo, w_hi)) + b_ref[...]


def _ada_call(cc, w_ada, b_ada):
    depth, d, d3 = w_ada.shape
    rows = cc.shape[0]
    tn = 512
    return pl.pallas_call(
        _ada_kernel,
        out_shape=jax.ShapeDtypeStruct((depth, rows, d3), F32),
        grid=(depth, d3 // tn),
        in_specs=[pl.BlockSpec((rows, d), lambda l, j: (0, 0)),
                  pl.BlockSpec((None, d, tn), lambda l, j: (l, 0, j)),
                  pl.BlockSpec((None, 1, tn), lambda l, j: (l, 0, j))],
        out_specs=pl.BlockSpec((None, rows, tn), lambda l, j: (l, 0, j)),
        compiler_params=_cparams(("parallel", "parallel")),
        name="ada_mod",
    )(cc, w_ada, b_ada.reshape(depth, 1, d3))


def _rope(t, cos_t, sin_t, lower):
    outs = []
    for j in range(t.shape[1] // LANES):
        tj = t[:, j * LANES:(j + 1) * LANES]
        partner = jnp.where(lower, pltpu.roll(tj, LANES - 32, 1), pltpu.roll(tj, 32, 1))
        outs.append(tj * cos_t + partner * sin_t)
    return jnp.concatenate(outs, axis=1)


def _in_kernel(x_ref, scale_ref, shift_ref, w_ref, cos_ref, sin_ref, cng_ref, ws_ref, bs_ref, *outs,
               rope, ctx_only, tm):
    x = x_ref[...]
    ms = jnp.mean(x * x, axis=-1, keepdims=True)
    h = (x * lax.rsqrt(ms + EPS)) * (1.0 + scale_ref[...]) + shift_ref[...]
    hb = h.astype(BF16)

    def proj(name):
        lo, hi = _PACK[name]
        return _dot(hb, w_ref[:, lo:hi])

    if ctx_only:
        xbc_ref, dt_ref, k_ref, v_ref = outs
    else:
        xbc_ref, dt_ref, k_ref, v_ref, ya_ref, z_ref, q_ref, gc_ref = outs

    xbc_ref[...] = proj("xbc")
    dt_ref[...] = proj("dt")
    v_ref[...] = proj("v").astype(BF16)
    k = proj("k")
    if rope:
        lane = lax.broadcasted_iota(jnp.int32, (tm, LANES), 1)
        lower = (lane & 32) == 0
        cos_t = cos_ref[...]
        sin_t = sin_ref[...]
        k = _rope(k, cos_t, sin_t, lower)
    k_ref[...] = k.astype(BF16)
    if ctx_only:
        return

    q = proj("q")
    if rope:
        q = _rope(q, cos_t, sin_t, lower)
    q_ref[...] = (q * (ATT_QK_DIM ** -0.5)).astype(BF16)
    z_ref[...] = proj("z")
    gc_ref[...] = proj("gc")

    u = jax.nn.gelu(proj("u"))
    vm = jax.nn.gelu(proj("vm"))
    vn = (vm * lax.rsqrt(jnp.mean(vm * vm, axis=-1, keepdims=True) + EPS)) * cng_ref[...]
    vnb = vn.astype(BF16)
    gate = _silu(proj("ga"))
    for r in range(tm // CHUNK):
        rows = slice(r * CHUNK, (r + 1) * CHUNK)
        for g in range(A_GROUPS):
            cols = slice(g * LANES, (g + 1) * LANES)
            mixed = _dot(ws_ref[g], vnb[rows, cols]) + bs_ref[:, cols]
            ya_ref[rows, cols] = (u[rows, cols] * mixed * gate[rows, cols]).astype(BF16)


def _in_call(x, scale, shift, w_pack, cos_t, sin_t, cng, ws_b, bs_exp, *, rope, ctx_only):
    b, n, d = x.shape
    tm = min(TM_IN, n)
    ncols = PACK_CTX_COLS if ctx_only else PACK_COLS
    per_batch = scale.shape[0] > 1
    mod_map = (lambda i, j: (i, 0, 0)) if per_batch else (lambda i, j: (0, 0, 0))
    tok = lambda width: pl.BlockSpec((None, tm, width), lambda i, j: (i, j, 0))
    const2 = lambda shape: pl.BlockSpec(shape, lambda i, j: (0, 0))
    out_shape = [jax.ShapeDtypeStruct((b, n, N_CONV_CH), F32), jax.ShapeDtypeStruct((b, n, LANES), F32),
                 jax.ShapeDtypeStruct((b, n, D_QK), BF16), jax.ShapeDtypeStruct((b, n, D_ATT), BF16)]
    out_specs = [tok(N_CONV_CH), tok(LANES), tok(D_QK), tok(D_ATT)]
    if not ctx_only:
        out_shape += [jax.ShapeDtypeStruct((b, n, D_CHUNK), BF16), jax.ShapeDtypeStruct((b, n, D_SSM), F32),
                      jax.ShapeDtypeStruct((b, n, D_QK), BF16), jax.ShapeDtypeStruct((b, n, D_ATT), F32)]
        out_specs += [tok(D_CHUNK), tok(D_SSM), tok(D_QK), tok(D_ATT)]
    return pl.pallas_call(
        functools.partial(_in_kernel, rope=rope, ctx_only=ctx_only, tm=tm),
        out_shape=out_shape,
        grid=(b, n // tm),
        in_specs=[tok(d),
                  pl.BlockSpec((None, 1, d), mod_map),
                  pl.BlockSpec((None, 1, d), mod_map),
                  pl.BlockSpec((d, ncols), lambda i, j: (0, 0)),
                  pl.BlockSpec((tm, LANES), lambda i, j: (j, 0)),
                  pl.BlockSpec((tm, LANES), lambda i, j: (j, 0)),
                  const2((1, D_CHUNK)),
                  pl.BlockSpec((A_GROUPS, CHUNK, CHUNK), lambda i, j: (0, 0, 0)),
                  const2((CHUNK, D_CHUNK))],
        out_specs=out_specs,
        compiler_params=_cparams(("parallel", "parallel")),
        name="in_proj_ctx" if ctx_only else "in_proj",
    )(x, scale, shift, w_pack, cos_t, sin_t, cng, ws_b, bs_exp)


HALO = SUBLANES


def _conv_silu(cur_ref, prev_ref, next_ref, has_prev, has_next, cw_ref, cb_ref, ext_ref):
    ext_ref[0:HALO, :] = jnp.where(has_prev, prev_ref[...], 0.0)
    ext_ref[HALO:HALO + SSM_CHUNK, :] = cur_ref[...]
    ext_ref[HALO + SSM_CHUNK:, :] = jnp.where(has_next, next_ref[...], 0.0)
    acc = jnp.broadcast_to(cb_ref[...], (SSM_CHUNK, N_CONV_CH))
    for t in range(SSM_CONV):
        acc = acc + ext_ref[pl.ds(HALO - SSM_CONV // 2 + t, SSM_CHUNK), :] * cw_ref[t:t + 1, :]
    return _silu(acc)


def _expand_heads(f, e_mat):
    lane = lax.broadcasted_iota(jnp.int32, f.shape, 1)
    hi, mid, lo = _split3(f)
    packed = jnp.where(lane < 32, hi.astype(F32),
                       jnp.where(lane < 64, pltpu.roll(mid.astype(F32), 32, 1),
                                 jnp.where(lane < 96, pltpu.roll(lo.astype(F32), 64, 1), 0.0)))
    return _dot(packed.astype(BF16), e_mat)


def _ssd_direction(d, xbc, dt_raw, dtb, aneg, mask, st_ref, e_mat):
    L = SSM_CHUNK
    lane = lax.broadcasted_iota(jnp.int32, (L, LANES), 1)
    x = xbc[:, :D_SSM]
    t = dt_raw + dtb
    dt = jnp.maximum(t, 0.0) + jnp.log1p(jnp.exp(-jnp.abs(t)))
    a = dt * aneg
    tri = jnp.where(mask, 1.0, 0.0).astype(BF16)
    hi, mid, lo = _split3(a)
    cs = _dot(tri, hi) + _dot(tri, mid) + _dot(tri, lo)
    tot = cs[L - 1:L, :] if d == 0 else cs[0:1, :]
    cs_t = cs.T
    dt_e = _expand_heads(dt, e_mat)
    w_e = _expand_heads(dt * jnp.exp(tot - cs), e_mat)
    sdec = _expand_heads(jnp.broadcast_to(jnp.exp(tot), (SUBLANES, LANES)), e_mat)[0:1, :]
    xdt = (x * dt_e).astype(BF16)
    xw = (x * w_e).astype(BF16)
    ys = []
    for g in range(SSM_GROUPS):
        b0 = D_SSM + g * SSM_STATE
        c0 = D_SSM + SSM_GROUPS * SSM_STATE + g * SSM_STATE
        bm = xbc[:, b0:b0 + SSM_STATE]
        cm = xbc[:, c0:c0 + SSM_STATE]
        cb = _dot_nt(cm.astype(BF16), bm.astype(BF16))
        st = st_ref[d, g]
        stb = st.astype(BF16)
        for hp in range(HEADS_PER_GROUP // 2):
            col0 = g * GROUP_W + hp * LANES
            rhs = jnp.concatenate([xdt[:, col0:col0 + LANES], stb[:, hp * LANES:(hp + 1) * LANES]], axis=0)
            pair = []
            for hh in range(2):
                hl = d * SSM_HEADS + g * HEADS_PER_GROUP + 2 * hp + hh
                colb = cs[:, hl:hl + 1]
                rowb = cs_t[hl:hl + 1, :]
                seg = jnp.where(mask, jnp.exp(colb - rowb), 0.0)
                lhs = jnp.concatenate([(cb * seg).astype(BF16), (cm * jnp.exp(colb)).astype(BF16)], axis=1)
                pair.append(_dot(lhs, rhs))
            ys.append(jnp.where(lane < SSM_HEAD_DIM, pair[0], pair[1]))
        gcols = slice(g * GROUP_W, (g + 1) * GROUP_W)
        st_ref[d, g] = st * sdec[:, gcols] + _dot(bm.T.astype(BF16), xw[:, gcols])
    return jnp.concatenate(ys, axis=1), x


def _ssd_kernel(fc_ref, fp_ref, fn_ref, bc_ref, bp_ref, bn_ref, dtf_ref, dtb_ref, cw_ref, cb_ref, dtbias_ref,
                alog_ref, dskip_ref, ef_ref, eb_ref, h0_ref, yf_ref, yb_ref, st_ref, ext_ref, *, nc):
    s = pl.program_id(1)

    @pl.when(s == 0)
    def _():
        st_ref[...] = h0_ref[...]

    row = lax.broadcasted_iota(jnp.int32, (SSM_CHUNK, SSM_CHUNK), 0)
    col = lax.broadcasted_iota(jnp.int32, (SSM_CHUNK, SSM_CHUNK), 1)
    aneg = -jnp.exp(alog_ref[...])
    dtbias = dtbias_ref[...]

    xbc_f = _conv_silu(fc_ref, fp_ref, fn_ref, s > 0, s < nc - 1, cw_ref, cb_ref, ext_ref)
    y_f, x_f = _ssd_direction(0, xbc_f, dtf_ref[...], dtbias, aneg, col <= row, st_ref, ef_ref[...])
    yf_ref[...] = y_f + dskip_ref[...] * x_f

    xbc_b = _conv_silu(bc_ref, bp_ref, bn_ref, s < nc - 1, s > 0, cw_ref, cb_ref, ext_ref)
    y_b, _ = _ssd_direction(1, xbc_b, dtb_ref[...], dtbias, aneg, col >= row, st_ref, eb_ref[...])
    yb_ref[...] = y_b


def _ssd_call(xbc, dt_raw, conv_w, conv_b, dt_bias_p, a_log_p, dskip_e, e_f, e_b, h0):
    b, n, _ = xbc.shape
    nc = n // SSM_CHUNK
    hpc = SSM_CHUNK // HALO
    last_halo = n // HALO - 1
    fwd = lambda i, s: (i, s, 0)
    bwd = lambda i, s: (i, nc - 1 - s, 0)
    chunk = lambda width, imap: pl.BlockSpec((None, SSM_CHUNK, width), imap)
    halo = lambda imap: pl.BlockSpec((None, HALO, N_CONV_CH), imap)
    const2 = lambda shape: pl.BlockSpec(shape, lambda i, s: (0, 0))
    state_spec = pl.BlockSpec((None, 2, SSM_GROUPS, SSM_STATE, GROUP_W), lambda i, s: (i, 0, 0, 0, 0))
    return pl.pallas_call(
        functools.partial(_ssd_kernel, nc=nc),
        out_shape=[jax.ShapeDtypeStruct((b, n, D_SSM), F32), jax.ShapeDtypeStruct((b, n, D_SSM), F32),
                   jax.ShapeDtypeStruct((b, 2, SSM_GROUPS, SSM_STATE, GROUP_W), F32)],
        grid=(b, nc),
        in_specs=[chunk(N_CONV_CH, fwd),
                  halo(lambda i, s: (i, jnp.maximum(s * hpc - 1, 0), 0)),
                  halo(lambda i, s: (i, jnp.minimum((s + 1) * hpc, last_halo), 0)),
                  chunk(N_CONV_CH, bwd),
                  halo(lambda i, s: (i, jnp.maximum((nc - 1 - s) * hpc - 1, 0), 0)),
                  halo(lambda i, s: (i, jnp.minimum((nc - s) * hpc, last_halo), 0)),
                  chunk(LANES, fwd),
                  chunk(LANES, bwd),
                  const2((SSM_CONV, N_CONV_CH)),
                  const2((1, N_CONV_CH)),
                  const2((1, LANES)),
                  const2((1, LANES)),
                  const2((1, D_SSM)),
                  const2((LANES, D_SSM)),
                  const2((LANES, D_SSM)),
                  state_spec],
        out_specs=[chunk(D_SSM, fwd), chunk(D_SSM, bwd), state_spec],
        scratch_shapes=[pltpu.VMEM((SSM_CHUNK + 2 * HALO, N_CONV_CH), F32)],
        compiler_params=_cparams(("parallel", "arbitrary")),
        name="ssd_bidir",
    )(xbc, xbc, xbc, xbc, xbc, xbc, dt_raw, dt_raw, conv_w, conv_b, dt_bias_p, a_log_p, dskip_e, e_f, e_b, h0)


def _attn_kernel(*refs, with_latent, n_lat, lam_init):
    if with_latent:
        q_ref, k_ref, v_ref, kc_ref, vc_ref, gc_ref, lamp_ref, sg_ref, o_ref, m_ref, l_ref, acc_ref = refs
    else:
        q_ref, kc_ref, vc_ref, gc_ref, lamp_ref, sg_ref, o_ref, m_ref, l_ref, acc_ref = refs
    tq = q_ref.shape[0]
    q = q_ref[...].astype(F32)
    lane = lax.broadcasted_iota(jnp.int32, (tq, LANES), 1)
    qz = (jnp.where(lane < ATT_QK_DIM, q, 0.0).astype(BF16), jnp.where(lane >= ATT_QK_DIM, q, 0.0).astype(BF16))

    m_ref[...] = jnp.full(m_ref.shape, -jnp.inf, F32)
    l_ref[...] = jnp.zeros(l_ref.shape, F32)
    acc_ref[...] = jnp.zeros(acc_ref.shape, F32)

    def step(k_t, v_t):
        ntile = k_t.shape[0] // LANES
        for i in range(2):
            s = _dot_nt(qz[i], k_t)
            tiles = [s[:, j * LANES:(j + 1) * LANES] for j in range(ntile)]
            smax = tiles[0]
            for tj in tiles[1:]:
                smax = jnp.maximum(smax, tj)
            m_prev = m_ref[i]
            m_new = jnp.maximum(m_prev, jnp.max(smax, axis=-1, keepdims=True))
            alpha = jnp.exp(m_prev - m_new)
            ps = [jnp.exp(tj - m_new) for tj in tiles]
            psum = ps[0]
            for pj in ps[1:]:
                psum = psum + pj
            l_ref[i] = alpha * l_ref[i] + psum
            p = jnp.concatenate(ps, axis=1).astype(BF16)
            acc_ref[i] = alpha * acc_ref[i] + _dot(p, v_t)
            m_ref[i] = m_new

    if with_latent:
        def body(j, carry):
            start = pl.multiple_of(j * TK, TK)
            step(k_ref[pl.ds(start, TK), :], v_ref[pl.ds(start, TK), :])
            return carry
        lax.fori_loop(0, n_lat // TK, body, 0)
    step(kc_ref[...], vc_ref[...])

    lp = lamp_ref[...]
    lam = (jnp.exp(jnp.sum(lp[0:1] * lp[1:2], axis=-1, keepdims=True))
           - jnp.exp(jnp.sum(lp[2:3] * lp[3:4], axis=-1, keepdims=True)) + lam_init)
    l1 = jnp.sum(l_ref[0], axis=-1, keepdims=True)
    l2 = jnp.sum(l_ref[1], axis=-1, keepdims=True)
    o = acc_ref[0] / l1 - lam * (acc_ref[1] / l2)
    o = (o * lax.rsqrt(jnp.mean(o * o, axis=-1, keepdims=True) + EPS)) * sg_ref[...] * (1.0 - lam_init)
    o_ref[...] = (o * _silu(gc_ref[...])).astype(BF16)


def _attn_call(q, k, v, kc, vc, gc, lam_p, subln_g, *, lam_init):
    b, n, _ = q.shape
    m = kc.shape[1]
    with_latent = k is not None
    tq = min(TQ, n)
    qmap = lambda i, h, j: (i, j, h)
    slab = lambda rows: pl.BlockSpec((None, rows, LANES), lambda i, h, j: (i, 0, h))
    in_specs = [pl.BlockSpec((None, tq, LANES), qmap)]
    args = [q]
    if with_latent:
        in_specs += [slab(n), slab(n)]
        args += [k, v]
    in_specs += [slab(m), slab(m), pl.BlockSpec((None, tq, LANES), qmap),
                 pl.BlockSpec((4, ATT_QK_DIM), lambda i, h, j: (0, 0)),
                 pl.BlockSpec((1, ATT_V_DIM), lambda i, h, j: (0, 0))]
    args += [kc, vc, gc, lam_p, subln_g]
    return pl.pallas_call(
        functools.partial(_attn_kernel, with_latent=with_latent, n_lat=n, lam_init=lam_init),
        out_shape=jax.ShapeDtypeStruct((b, n, D_ATT), BF16),
        grid=(b, ATT_HEADS, n // tq),
        in_specs=in_specs,
        out_specs=pl.BlockSpec((None, tq, LANES), qmap),
        scratch_shapes=[pltpu.VMEM((2, tq, LANES), F32), pltpu.VMEM((2, tq, LANES), F32),
                        pltpu.VMEM((2, tq, LANES), F32)],
        compiler_params=_cparams(("parallel", "parallel", "arbitrary")),
        name="diff_attn" if with_latent else "diff_attn_ctx",
    )(*args)


def _out_kernel(ya_ref, yf_ref, yb_ref, z_ref, yc_ref, x_ref, gate_ref, ng_ref, wa_ref, wb_ref, wc_ref, fg_ref,
                o_ref, *, final_norm):
    y = (yf_ref[...] + yb_ref[...]) * _silu(z_ref[...])
    parts = []
    for g in range(SSM_GROUPS):
        yg = y[:, g * GROUP_W:(g + 1) * GROUP_W]
        parts.append(yg * lax.rsqrt(jnp.mean(yg * yg, axis=-1, keepdims=True) + EPS))
    yn = (jnp.concatenate(parts, axis=1) * ng_ref[...]).astype(BF16)
    mix = _dot(ya_ref[...], wa_ref[...]) + _dot(yn, wb_ref[...]) + _dot(yc_ref[...], wc_ref[...])
    xo = x_ref[...] + gate_ref[...] * mix
    if final_norm:
        xo = (xo * lax.rsqrt(jnp.mean(xo * xo, axis=-1, keepdims=True) + EPS)) * fg_ref[...]
    o_ref[...] = xo


def _out_call(ya, yf, yb, z, yc, x, gate, norm_g, w_a, w_b, w_c, final_g, *, final_norm):
    b, n, d = x.shape
    tm = min(TM_OUT, n)
    per_batch = gate.shape[0] > 1
    mod_map = (lambda i, j: (i, 0, 0)) if per_batch else (lambda i, j: (0, 0, 0))
    tok = lambda width: pl.BlockSpec((None, tm, width), lambda i, j: (i, j, 0))
    const2 = lambda shape: pl.BlockSpec(shape, lambda i, j: (0, 0))
    return pl.pallas_call(
        functools.partial(_out_kernel, final_norm=final_norm),
        out_shape=jax.ShapeDtypeStruct((b, n, d), F32),
        grid=(b, n // tm),
        in_specs=[tok(D_CHUNK), tok(D_SSM), tok(D_SSM), tok(D_SSM), tok(D_ATT), tok(d),
                  pl.BlockSpec((None, 1, d), mod_map),
                  const2((1, D_SSM)), const2((D_CHUNK, d)), const2((D_SSM, d)), const2((D_ATT, d)),
                  const2((1, d))],
        out_specs=tok(d),
        compiler_params=_cparams(("parallel", "parallel")),
        name="out_proj",
    )(ya, yf, yb, z, yc, x, gate, norm_g, w_a, w_b, w_c, final_g)


def _rope_tables(n):
    rows = n // GRID_W
    row = jnp.repeat(jnp.arange(rows), GRID_W).astype(F32)
    col = jnp.tile(jnp.arange(GRID_W), rows).astype(F32)
    n_freq = ATT_QK_DIM // 4
    inv = ROPE_THETA ** (-jnp.arange(n_freq, dtype=F32) / n_freq)
    ang = jnp.concatenate([row[:, None] * inv, col[:, None] * inv], axis=-1)
    cos, sin = jnp.cos(ang), jnp.sin(ang)
    cos_t = jnp.tile(cos, (1, LANES // (ATT_QK_DIM // 2)))
    sin_t = jnp.tile(jnp.concatenate([-sin, sin], axis=-1), (1, LANES // ATT_QK_DIM))
    return cos_t, sin_t


def _head_expanders():
    k = jnp.arange(LANES)[:, None]
    head = jnp.arange(D_SSM)[None, :] // SSM_HEAD_DIM
    used = k < 3 * N_DT
    e_f = (used & ((k % N_DT) == head)).astype(BF16)
    e_b = (used & ((k % N_DT) == head + SSM_HEADS)).astype(BF16)
    return e_f, e_b


def _pack_w_in(w):
    dt0, dt1 = N_CONV_CH, N_CONV_CH + N_DT
    pad = jnp.zeros((w.shape[0], LANES - N_DT), w.dtype)
    return jnp.concatenate([w[:, :dt0], w[:, dt0:dt1], pad, w[:, dt1:]], axis=1).astype(BF16)


def _pad_lanes(v):
    flat = v.reshape(1, -1)
    return jnp.pad(flat, ((0, 0), (0, LANES - flat.shape[1])))


def kernel(x, c, ctx, c_ctx, w_ada, b_ada, w_in, w_out, chunk_norm_g, chunk_ws, chunk_bs, ssm_conv_w, ssm_conv_b,
           ssm_dt_bias, ssm_a_log, ssm_d, ssm_norm_g, diff_lambda_p, diff_subln_g, final_norm_g):
    depth = w_in.shape[0]
    b, n, d = x.shape
    m = ctx.shape[1]

    rows = -(-(b + 1) // SUBLANES) * SUBLANES
    cc = jnp.concatenate([c, c_ctx[None, :], jnp.zeros((rows - b - 1, d), F32)], axis=0)
    mod = _ada_call(cc, w_ada, b_ada)

    cos_t, sin_t = _rope_tables(n)
    e_f, e_b = _head_expanders()
    h_zero = jnp.zeros((b, 2, SSM_GROUPS, SSM_STATE, GROUP_W), F32)
    final_g = final_norm_g.reshape(1, d)

    xc = ctx
    for li in range(depth):
        last = li == depth - 1
        lam_init = 0.8 - 0.6 * math.exp(-0.3 * li)
        shift, scale, gate = (mod[li, :b, i * d:(i + 1) * d].reshape(b, 1, d) for i in range(3))
        shift_c, scale_c, gate_c = (mod[li, b:b + 1, i * d:(i + 1) * d].reshape(1, 1, d) for i in range(3))
        w_pack = _pack_w_in(w_in[li])
        w_o = w_out[li].astype(BF16)
        w_a, w_b, w_c = w_o[:D_CHUNK], w_o[D_CHUNK:D_CHUNK + D_SSM], w_o[D_CHUNK + D_SSM:]
        cng = chunk_norm_g[li].reshape(1, D_CHUNK)
        ws_b = chunk_ws[li].astype(BF16)
        bs_exp = jnp.repeat(chunk_bs[li].T, D_CHUNK // A_GROUPS, axis=1)
        conv_w = ssm_conv_w[li]
        conv_b = ssm_conv_b[li].reshape(1, N_CONV_CH)
        dt_bias_p = _pad_lanes(ssm_dt_bias[li])
        a_log_p = _pad_lanes(ssm_a_log[li])
        dskip_e = jnp.repeat(ssm_d[li], SSM_HEAD_DIM).reshape(1, D_SSM)
        norm_g = ssm_norm_g[li].reshape(1, D_SSM)
        lam_p = diff_lambda_p[li]
        subln_g = diff_subln_g[li].reshape(1, ATT_V_DIM)

        c_out = _in_call(xc, scale_c, shift_c, w_pack, cos_t, sin_t, cng, ws_b, bs_exp, rope=False, ctx_only=last)
        xbc_c, dt_c, k_c, v_c = c_out[:4]
        yf_c, yb_c, h_c = _ssd_call(xbc_c, dt_c, conv_w, conv_b, dt_bias_p, a_log_p, dskip_e, e_f, e_b, h_zero)
        if not last:
            ya_c, z_c, q_c, gc_c = c_out[4:]
            yc_c = _attn_call(q_c, None, None, k_c, v_c, gc_c, lam_p, subln_g, lam_init=lam_init)
            xc_new = _out_call(ya_c, yf_c, yb_c, z_c, yc_c, xc, gate_c, norm_g, w_a, w_b, w_c, final_g,
                               final_norm=False)

        xbc, dt_raw, k, v, ya, z, q, gc = _in_call(x, scale, shift, w_pack, cos_t, sin_t, cng, ws_b, bs_exp,
                                                   rope=True, ctx_only=False)
        yf, yb, _ = _ssd_call(xbc, dt_raw, conv_w, conv_b, dt_bias_p, a_log_p, dskip_e, e_f, e_b, h_c)
        yc = _attn_call(q, k, v, k_c, v_c, gc, lam_p, subln_g, lam_init=lam_init)
        x = _out_call(ya, yf, yb, z, yc, x, gate, norm_g, w_a, w_b, w_c, final_g, final_norm=last)
        if not last:
            xc = xc_new
    return x
```

```python
import functools
import math

import jax
import jax.numpy as jnp
from jax import lax
from jax.experimental import pallas as pl
from jax.experimental.pallas import tpu as pltpu

F32 = jnp.float32
BF16 = jnp.bfloat16

EPS = 1e-6
LANES = 128
SUBLANES = 8
V7X_VMEM_BYTES = 64 * 1024 * 1024
VMEM_LIMIT = V7X_VMEM_BYTES * 3 // 4

D_MODEL = 1024
D_CHUNK = 512
CHUNK = 128
A_GROUPS = 4
D_SSM = 1024
SSM_HEADS = 16
SSM_HEAD_DIM = 64
SSM_GROUPS = 2
SSM_STATE = 128
SSM_CONV = 5
SSM_CHUNK = 128
GROUP_W = D_SSM // SSM_GROUPS
HEADS_PER_GROUP = SSM_HEADS // SSM_GROUPS
D_ATT = 512
ATT_QK_DIM = 64
ATT_V_DIM = 128
ATT_HEADS = 4
D_QK = 512
ROPE_THETA = 10000.0
GRID_W = 64
N_CONV_CH = D_SSM + 2 * SSM_GROUPS * SSM_STATE
N_DT = 2 * SSM_HEADS
D_MIX = D_CHUNK + D_SSM + D_ATT

_PACK_SIZES = (("xbc", N_CONV_CH), ("dt", LANES), ("k", D_QK), ("v", D_ATT), ("u", D_CHUNK), ("vm", D_CHUNK),
               ("ga", D_CHUNK), ("z", D_SSM), ("q", D_QK), ("gc", D_ATT))
_PACK = {}
_off = 0
for _name, _size in _PACK_SIZES:
    _PACK[_name] = (_off, _off + _size)
    _off += _size
PACK_COLS = _off
PACK_CTX_COLS = _PACK["v"][1]

Q_SCALE = ATT_QK_DIM ** -0.5 * math.log2(math.e)

TM_IN = 256
TM_CONV = 512
TM_OUT = 512
TQ = 256
TK = 512


def _cparams(semantics):
    return pltpu.CompilerParams(dimension_semantics=semantics, vmem_limit_bytes=VMEM_LIMIT)


def _silu(t):
    return t * jax.nn.sigmoid(t)


def _split3(a):
    hi = a.astype(BF16)
    r = a - hi.astype(F32)
    mid = r.astype(BF16)
    lo = (r - mid.astype(F32)).astype(BF16)
    return hi, mid, lo


def _dot(a, b):
    return jnp.dot(a, b, preferred_element_type=F32)


def _dot_nt(a, b):
    return lax.dot_general(a, b, (((1,), (1,)), ((), ())), preferred_element_type=F32)


def _ada_kernel(c_ref, w_ref, b_ref, o_ref):
    s = _silu(c_ref[...])
    w = w_ref[...]
    s_hi = s.astype(BF16)
    s_lo = (s - s_hi.astype(F32)).astype(BF16)
    w_hi = w.astype(BF16)
    w_lo = (w - w_hi.astype(F32)).astype(BF16)
    o_ref[...] = _dot(s_hi, w_hi) + (_dot(s_hi, w_lo) + _dot(s_lo, w_hi)) + b_ref[...]


def _ada_call(cc, w_ada, b_ada):
    depth, d, d3 = w_ada.shape
    rows = cc.shape[0]
    tn = 512
    return pl.pallas_call(
        _ada_kernel,
        out_shape=jax.ShapeDtypeStruct((depth, rows, d3), F32),
        grid=(depth, d3 // tn),
        in_specs=[pl.BlockSpec((rows, d), lambda l, j: (0, 0)),
                  pl.BlockSpec((None, d, tn), lambda l, j: (l, 0, j)),
                  pl.BlockSpec((None, 1, tn), lambda l, j: (l, 0, j))],
        out_specs=pl.BlockSpec((None, rows, tn), lambda l, j: (l, 0, j)),
        compiler_params=_cparams(("parallel", "parallel")),
        name="ada_mod",
    )(cc, w_ada, b_ada.reshape(depth, 1, d3))


def _rope(t, cos_t, sin_t, lower):
    outs = []
    for j in range(t.shape[1] // LANES):
        tj = t[:, j * LANES:(j + 1) * LANES]
        partner = jnp.where(lower, pltpu.roll(tj, LANES - 32, 1), pltpu.roll(tj, 32, 1))
        outs.append(tj * cos_t + partner * sin_t)
    return jnp.concatenate(outs, axis=1)


def _in_kernel(x_ref, scale_ref, shift_ref, w_ref, cos_ref, sin_ref, cng_ref, ws_ref, bs_ref, *outs,
               rope, ctx_only, tm):
    x = x_ref[...]
    ms = jnp.mean(x * x, axis=-1, keepdims=True)
    h = (x * lax.rsqrt(ms + EPS)) * (1.0 + scale_ref[...]) + shift_ref[...]
    hb = h.astype(BF16)

    def proj(name):
        lo, hi = _PACK[name]
        return _dot(hb, w_ref[:, lo:hi])

    if ctx_only:
        xbc_ref, dt_ref, k_ref, v_ref = outs
    else:
        xbc_ref, dt_ref, k_ref, v_ref, ya_ref, z_ref, q_ref, gc_ref = outs

    xbc_ref[...] = proj("xbc")
    dt_ref[...] = proj("dt")
    v_ref[...] = proj("v").astype(BF16)
    k = proj("k")
    if rope:
        lane = lax.broadcasted_iota(jnp.int32, (tm, LANES), 1)
        lower = (lane & 32) == 0
        cos_t = cos_ref[...]
        sin_t = sin_ref[...]
        k = _rope(k, cos_t, sin_t, lower)
    k_ref[...] = k.astype(BF16)
    if ctx_only:
        return

    q = proj("q")
    if rope:
        q = _rope(q, cos_t, sin_t, lower)
    q_ref[...] = (q * Q_SCALE).astype(BF16)
    z_ref[...] = proj("z")
    gc_ref[...] = proj("gc")

    u = jax.nn.gelu(proj("u"))
    vm = jax.nn.gelu(proj("vm"))
    vn = (vm * lax.rsqrt(jnp.mean(vm * vm, axis=-1, keepdims=True) + EPS)) * cng_ref[...]
    vnb = vn.astype(BF16)
    gate = _silu(proj("ga"))
    for r in range(tm // CHUNK):
        rows = slice(r * CHUNK, (r + 1) * CHUNK)
        for g in range(A_GROUPS):
            cols = slice(g * LANES, (g + 1) * LANES)
            mixed = _dot(ws_ref[g], vnb[rows, cols]) + bs_ref[:, cols]
            ya_ref[rows, cols] = (u[rows, cols] * mixed * gate[rows, cols]).astype(BF16)


def _in_call(x, scale, shift, w_pack, cos_t, sin_t, cng, ws_b, bs_exp, *, rope, ctx_only):
    b, n, d = x.shape
    tm = min(TM_IN, n)
    ncols = PACK_CTX_COLS if ctx_only else PACK_COLS
    per_batch = scale.shape[0] > 1
    mod_map = (lambda i, j: (i, 0, 0)) if per_batch else (lambda i, j: (0, 0, 0))
    tok = lambda width: pl.BlockSpec((None, tm, width), lambda i, j: (i, j, 0))
    const2 = lambda shape: pl.BlockSpec(shape, lambda i, j: (0, 0))
    out_shape = [jax.ShapeDtypeStruct((b, n, N_CONV_CH), F32), jax.ShapeDtypeStruct((b, n, LANES), F32),
                 jax.ShapeDtypeStruct((b, n, D_QK), BF16), jax.ShapeDtypeStruct((b, n, D_ATT), BF16)]
    out_specs = [tok(N_CONV_CH), tok(LANES), tok(D_QK), tok(D_ATT)]
    if not ctx_only:
        out_shape += [jax.ShapeDtypeStruct((b, n, D_CHUNK), BF16), jax.ShapeDtypeStruct((b, n, D_SSM), F32),
                      jax.ShapeDtypeStruct((b, n, D_QK), BF16), jax.ShapeDtypeStruct((b, n, D_ATT), F32)]
        out_specs += [tok(D_CHUNK), tok(D_SSM), tok(D_QK), tok(D_ATT)]
    return pl.pallas_call(
        functools.partial(_in_kernel, rope=rope, ctx_only=ctx_only, tm=tm),
        out_shape=out_shape,
        grid=(b, n // tm),
        in_specs=[tok(d),
                  pl.BlockSpec((None, 1, d), mod_map),
                  pl.BlockSpec((None, 1, d), mod_map),
                  pl.BlockSpec((d, ncols), lambda i, j: (0, 0)),
                  pl.BlockSpec((tm, LANES), lambda i, j: (j, 0)),
                  pl.BlockSpec((tm, LANES), lambda i, j: (j, 0)),
                  const2((1, D_CHUNK)),
                  pl.BlockSpec((A_GROUPS, CHUNK, CHUNK), lambda i, j: (0, 0, 0)),
                  const2((CHUNK, D_CHUNK))],
        out_specs=out_specs,
        compiler_params=_cparams(("parallel", "parallel")),
        name="in_proj_ctx" if ctx_only else "in_proj",
    )(x, scale, shift, w_pack, cos_t, sin_t, cng, ws_b, bs_exp)


HALO = SUBLANES


def _conv_kernel(cur_ref, prev_ref, next_ref, cw_ref, cb_ref, xs_ref, bc_ref, *, nt):
    j = pl.program_id(1)
    tm = cur_ref.shape[0]
    rows = tm + 2 * HALO
    prev = jnp.where(j > 0, prev_ref[...], 0.0)
    nxt = jnp.where(j < nt - 1, next_ref[...], 0.0)
    for c in range(N_CONV_CH // LANES):
        cols = slice(c * LANES, (c + 1) * LANES)
        ext = jnp.concatenate([prev[:, cols], cur_ref[:, cols], nxt[:, cols]], axis=0)
        acc = cb_ref[:, cols] + ext[HALO:HALO + tm] * cw_ref[SSM_CONV // 2:SSM_CONV // 2 + 1, cols]
        for t in range(SSM_CONV):
            if t != SSM_CONV // 2:
                shifted = pltpu.roll(ext, (SSM_CONV // 2 - t) % rows, 0)
                acc = acc + shifted[HALO:HALO + tm] * cw_ref[t:t + 1, cols]
        y = _silu(acc)
        if c < D_SSM // LANES:
            xs_ref[:, cols] = y
        else:
            bc_ref[:, c * LANES - D_SSM:(c + 1) * LANES - D_SSM] = y.astype(BF16)


def _conv_call(xbc, conv_w, conv_b):
    b, n, _ = xbc.shape
    tm = min(TM_CONV, n)
    nt = n // tm
    hpt = tm // HALO
    last_halo = n // HALO - 1
    halo = lambda imap: pl.BlockSpec((None, HALO, N_CONV_CH), imap)
    return pl.pallas_call(
        functools.partial(_conv_kernel, nt=nt),
        out_shape=[jax.ShapeDtypeStruct((b, n, D_SSM), F32),
                   jax.ShapeDtypeStruct((b, n, N_CONV_CH - D_SSM), BF16)],
        grid=(b, nt),
        in_specs=[pl.BlockSpec((None, tm, N_CONV_CH), lambda i, j: (i, j, 0)),
                  halo(lambda i, j: (i, jnp.maximum(j * hpt - 1, 0), 0)),
                  halo(lambda i, j: (i, jnp.minimum((j + 1) * hpt, last_halo), 0)),
                  pl.BlockSpec((SSM_CONV, N_CONV_CH), lambda i, j: (0, 0)),
                  pl.BlockSpec((1, N_CONV_CH), lambda i, j: (0, 0))],
        out_specs=[pl.BlockSpec((None, tm, D_SSM), lambda i, j: (i, j, 0)),
                   pl.BlockSpec((None, tm, N_CONV_CH - D_SSM), lambda i, j: (i, j, 0))],
        compiler_params=_cparams(("parallel", "parallel")),
        name="ssd_conv",
    )(xbc, xbc, xbc, conv_w, conv_b)


def _expand_heads(f, e_mat):
    lane = lax.broadcasted_iota(jnp.int32, f.shape, 1)
    hi, mid, lo = _split3(f)
    packed = jnp.where(lane < 32, hi.astype(F32),
                       jnp.where(lane < 64, pltpu.roll(mid.astype(F32), 32, 1),
                                 jnp.where(lane < 96, pltpu.roll(lo.astype(F32), 64, 1), 0.0)))
    return _dot(packed.astype(BF16), e_mat)


def _ssd_direction(d, x, bc, dt_raw, dtb, aneg, mask, st_ref, e_mat):
    L = SSM_CHUNK
    lane = lax.broadcasted_iota(jnp.int32, (L, LANES), 1)
    t = dt_raw + dtb
    dt = jnp.maximum(t, 0.0) + jnp.log1p(jnp.exp(-jnp.abs(t)))
    a = dt * aneg
    tri = jnp.where(mask, 1.0, 0.0).astype(BF16)
    hi, mid, lo = _split3(a)
    cs = _dot(tri, hi) + _dot(tri, mid) + _dot(tri, lo)
    tot = cs[L - 1:L, :] if d == 0 else cs[0:1, :]
    cs_t = cs.T
    dt_e = _expand_heads(dt, e_mat)
    w_e = _expand_heads(dt * jnp.exp(tot - cs), e_mat)
    ecs_e = _expand_heads(jnp.exp(cs), e_mat)
    sdec = _expand_heads(jnp.broadcast_to(jnp.exp(tot), (SUBLANES, LANES)), e_mat)[0:1, :]
    xdt = (x * dt_e).astype(BF16)
    xw = (x * w_e).astype(BF16)
    ys = []
    for g in range(SSM_GROUPS):
        bm = bc[:, g * SSM_STATE:(g + 1) * SSM_STATE]
        cm = bc[:, (SSM_GROUPS + g) * SSM_STATE:(SSM_GROUPS + g + 1) * SSM_STATE]
        cb = _dot_nt(cm, bm)
        st = st_ref[d, g]
        gcols = slice(g * GROUP_W, (g + 1) * GROUP_W)
        y_off = _dot(cm, st.astype(BF16)) * ecs_e[:, gcols]
        for hp in range(HEADS_PER_GROUP // 2):
            col0 = g * GROUP_W + hp * LANES
            rhs = xdt[:, col0:col0 + LANES]
            pair = []
            for hh in range(2):
                hl = d * SSM_HEADS + g * HEADS_PER_GROUP + 2 * hp + hh
                seg = jnp.where(mask, jnp.exp(cs[:, hl:hl + 1] - cs_t[hl:hl + 1, :]), 0.0)
                pair.append(_dot((cb * seg).astype(BF16), rhs))
            ys.append(jnp.where(lane < SSM_HEAD_DIM, pair[0], pair[1]) + y_off[:, hp * LANES:(hp + 1) * LANES])
        st_ref[d, g] = st * sdec[:, gcols] + _dot(bm.astype(F32).T.astype(BF16), xw[:, gcols])
    return jnp.concatenate(ys, axis=1)


def _ssd_kernel(xf_ref, bcf_ref, dtf_ref, xb_ref, bcb_ref, dtb_ref, dtbias_ref, alog_ref, dskip_ref, ef_ref, eb_ref,
                h0_ref, yf_ref, yb_ref, st_ref):
    @pl.when(pl.program_id(1) == 0)
    def _():
        st_ref[...] = h0_ref[...]

    row = lax.broadcasted_iota(jnp.int32, (SSM_CHUNK, SSM_CHUNK), 0)
    col = lax.broadcasted_iota(jnp.int32, (SSM_CHUNK, SSM_CHUNK), 1)
    aneg = -jnp.exp(alog_ref[...])
    dtbias = dtbias_ref[...]

    x_f = xf_ref[...]
    y_f = _ssd_direction(0, x_f, bcf_ref[...], dtf_ref[...], dtbias, aneg, col <= row, st_ref, ef_ref[...])
    yf_ref[...] = y_f + dskip_ref[...] * x_f
    yb_ref[...] = _ssd_direction(1, xb_ref[...], bcb_ref[...], dtb_ref[...], dtbias, aneg, col >= row, st_ref,
                                 eb_ref[...])


def _ssd_call(xs, bc, dt_raw, dt_bias_p, a_log_p, dskip_e, e_f, e_b, h0):
    b, n, _ = xs.shape
    nc = n // SSM_CHUNK
    fwd = lambda i, s: (i, s, 0)
    bwd = lambda i, s: (i, nc - 1 - s, 0)
    chunk = lambda width, imap: pl.BlockSpec((None, SSM_CHUNK, width), imap)
    const2 = lambda shape: pl.BlockSpec(shape, lambda i, s: (0, 0))
    state_spec = pl.BlockSpec((None, 2, SSM_GROUPS, SSM_STATE, GROUP_W), lambda i, s: (i, 0, 0, 0, 0))
    bc_w = N_CONV_CH - D_SSM
    return pl.pallas_call(
        _ssd_kernel,
        out_shape=[jax.ShapeDtypeStruct((b, n, D_SSM), F32), jax.ShapeDtypeStruct((b, n, D_SSM), F32),
                   jax.ShapeDtypeStruct((b, 2, SSM_GROUPS, SSM_STATE, GROUP_W), F32)],
        grid=(b, nc),
        in_specs=[chunk(D_SSM, fwd), chunk(bc_w, fwd), chunk(LANES, fwd),
                  chunk(D_SSM, bwd), chunk(bc_w, bwd), chunk(LANES, bwd),
                  const2((1, LANES)), const2((1, LANES)), const2((1, D_SSM)),
                  const2((LANES, D_SSM)), const2((LANES, D_SSM)),
                  state_spec],
        out_specs=[chunk(D_SSM, fwd), chunk(D_SSM, bwd), state_spec],
        compiler_params=_cparams(("parallel", "arbitrary")),
        name="ssd_bidir",
    )(xs, bc, dt_raw, xs, bc, dt_raw, dt_bias_p, a_log_p, dskip_e, e_f, e_b, h0)


def _attn_kernel(*refs, with_latent, n_lat, lam_init):
    if with_latent:
        q_ref, k_ref, v_ref, kc_ref, vc_ref, gc_ref, lamp_ref, sg_ref, o_ref = refs
    else:
        q_ref, kc_ref, vc_ref, gc_ref, lamp_ref, sg_ref, o_ref = refs
    tq = q_ref.shape[0]
    q = q_ref[...].astype(F32)
    lane = lax.broadcasted_iota(jnp.int32, (tq, LANES), 1)
    qz = jnp.concatenate([jnp.where(lane < ATT_QK_DIM, q, 0.0), jnp.where(lane >= ATT_QK_DIM, q, 0.0)],
                         axis=0).astype(BF16)

    tiles = []
    if with_latent:
        tiles += [(k_ref, v_ref, j * TK, TK) for j in range(n_lat // TK)]
    tiles.append((kc_ref, vc_ref, 0, kc_ref.shape[0]))

    def scores(tile):
        kr, _, start, size = tile
        return _dot_nt(qz, kr[start:start + size, :])

    m_run = jnp.full((2 * tq, LANES), -jnp.inf, F32)
    l_run = jnp.zeros((2 * tq, LANES), F32)
    acc = jnp.zeros((2 * tq, LANES), F32)
    s_next = scores(tiles[0])
    for j, tile in enumerate(tiles):
        s = s_next
        if j + 1 < len(tiles):
            s_next = scores(tiles[j + 1])
        _, vr, start, size = tile
        parts = [s[:, i * LANES:(i + 1) * LANES] for i in range(size // LANES)]
        smax = parts[0]
        for pj in parts[1:]:
            smax = jnp.maximum(smax, pj)
        m_new = jnp.maximum(m_run, jnp.max(smax, axis=-1, keepdims=True))
        alpha = jnp.exp2(m_run - m_new)
        ps = [jnp.exp2(pj - m_new) for pj in parts]
        psum = ps[0]
        for pj in ps[1:]:
            psum = psum + pj
        l_run = alpha * l_run + psum
        p = jnp.concatenate(ps, axis=1).astype(BF16)
        acc = alpha * acc + _dot(p, vr[start:start + size, :])
        m_run = m_new

    lp = lamp_ref[...]
    lam = (jnp.exp(jnp.sum(lp[0:1] * lp[1:2], axis=-1, keepdims=True))
           - jnp.exp(jnp.sum(lp[2:3] * lp[3:4], axis=-1, keepdims=True)) + lam_init)
    o_maps = acc / jnp.sum(l_run, axis=-1, keepdims=True)
    o = o_maps[:tq] - lam * o_maps[tq:]
    o = (o * lax.rsqrt(jnp.mean(o * o, axis=-1, keepdims=True) + EPS)) * sg_ref[...] * (1.0 - lam_init)
    o_ref[...] = (o * _silu(gc_ref[...])).astype(BF16)


def _attn_call(q, k, v, kc, vc, gc, lam_p, subln_g, *, lam_init):
    b, n, _ = q.shape
    m = kc.shape[1]
    with_latent = k is not None
    tq = min(TQ, n)
    qmap = lambda i, h, j: (i, j, h)
    slab = lambda rows: pl.BlockSpec((None, rows, LANES), lambda i, h, j: (i, 0, h))
    in_specs = [pl.BlockSpec((None, tq, LANES), qmap)]
    args = [q]
    if with_latent:
        in_specs += [slab(n), slab(n)]
        args += [k, v]
    in_specs += [slab(m), slab(m), pl.BlockSpec((None, tq, LANES), qmap),
                 pl.BlockSpec((4, ATT_QK_DIM), lambda i, h, j: (0, 0)),
                 pl.BlockSpec((1, ATT_V_DIM), lambda i, h, j: (0, 0))]
    args += [kc, vc, gc, lam_p, subln_g]
    return pl.pallas_call(
        functools.partial(_attn_kernel, with_latent=with_latent, n_lat=n, lam_init=lam_init),
        out_shape=jax.ShapeDtypeStruct((b, n, D_ATT), BF16),
        grid=(b, ATT_HEADS, n // tq),
        in_specs=in_specs,
        out_specs=pl.BlockSpec((None, tq, LANES), qmap),
        compiler_params=_cparams(("parallel", "parallel", "arbitrary")),
        name="diff_attn" if with_latent else "diff_attn_ctx",
    )(*args)


def _out_kernel(ya_ref, yf_ref, yb_ref, z_ref, yc_ref, x_ref, gate_ref, ng_ref, wa_ref, wb_ref, wc_ref, fg_ref,
                o_ref, *, final_norm):
    y = (yf_ref[...] + yb_ref[...]) * _silu(z_ref[...])
    parts = []
    for g in range(SSM_GROUPS):
        yg = y[:, g * GROUP_W:(g + 1) * GROUP_W]
        parts.append(yg * lax.rsqrt(jnp.mean(yg * yg, axis=-1, keepdims=True) + EPS))
    yn = (jnp.concatenate(parts, axis=1) * ng_ref[...]).astype(BF16)
    mix = _dot(ya_ref[...], wa_ref[...]) + _dot(yn, wb_ref[...]) + _dot(yc_ref[...], wc_ref[...])
    xo = x_ref[...] + gate_ref[...] * mix
    if final_norm:
        xo = (xo * lax.rsqrt(jnp.mean(xo * xo, axis=-1, keepdims=True) + EPS)) * fg_ref[...]
    o_ref[...] = xo


def _out_call(ya, yf, yb, z, yc, x, gate, norm_g, w_a, w_b, w_c, final_g, *, final_norm):
    b, n, d = x.shape
    tm = min(TM_OUT, n)
    per_batch = gate.shape[0] > 1
    mod_map = (lambda i, j: (i, 0, 0)) if per_batch else (lambda i, j: (0, 0, 0))
    tok = lambda width: pl.BlockSpec((None, tm, width), lambda i, j: (i, j, 0))
    const2 = lambda shape: pl.BlockSpec(shape, lambda i, j: (0, 0))
    return pl.pallas_call(
        functools.partial(_out_kernel, final_norm=final_norm),
        out_shape=jax.ShapeDtypeStruct((b, n, d), F32),
        grid=(b, n // tm),
        in_specs=[tok(D_CHUNK), tok(D_SSM), tok(D_SSM), tok(D_SSM), tok(D_ATT), tok(d),
                  pl.BlockSpec((None, 1, d), mod_map),
                  const2((1, D_SSM)), const2((D_CHUNK, d)), const2((D_SSM, d)), const2((D_ATT, d)),
                  const2((1, d))],
        out_specs=tok(d),
        compiler_params=_cparams(("parallel", "parallel")),
        name="out_proj",
    )(ya, yf, yb, z, yc, x, gate, norm_g, w_a, w_b, w_c, final_g)


def _rope_tables(n):
    rows = n // GRID_W
    row = jnp.repeat(jnp.arange(rows), GRID_W).astype(F32)
    col = jnp.tile(jnp.arange(GRID_W), rows).astype(F32)
    n_freq = ATT_QK_DIM // 4
    inv = ROPE_THETA ** (-jnp.arange(n_freq, dtype=F32) / n_freq)
    ang = jnp.concatenate([row[:, None] * inv, col[:, None] * inv], axis=-1)
    cos, sin = jnp.cos(ang), jnp.sin(ang)
    cos_t = jnp.tile(cos, (1, LANES // (ATT_QK_DIM // 2)))
    sin_t = jnp.tile(jnp.concatenate([-sin, sin], axis=-1), (1, LANES // ATT_QK_DIM))
    return cos_t, sin_t


def _head_expanders():
    k = jnp.arange(LANES)[:, None]
    head = jnp.arange(D_SSM)[None, :] // SSM_HEAD_DIM
    used = k < 3 * N_DT
    e_f = (used & ((k % N_DT) == head)).astype(BF16)
    e_b = (used & ((k % N_DT) == head + SSM_HEADS)).astype(BF16)
    return e_f, e_b


def _pack_w_in(w):
    dt0, dt1 = N_CONV_CH, N_CONV_CH + N_DT
    pad = jnp.zeros((w.shape[0], LANES - N_DT), w.dtype)
    return jnp.concatenate([w[:, :dt0], w[:, dt0:dt1], pad, w[:, dt1:]], axis=1).astype(BF16)


def _pad_lanes(v):
    flat = v.reshape(1, -1)
    return jnp.pad(flat, ((0, 0), (0, LANES - flat.shape[1])))


def kernel(x, c, ctx, c_ctx, w_ada, b_ada, w_in, w_out, chunk_norm_g, chunk_ws, chunk_bs, ssm_conv_w, ssm_conv_b,
           ssm_dt_bias, ssm_a_log, ssm_d, ssm_norm_g, diff_lambda_p, diff_subln_g, final_norm_g):
    depth = w_in.shape[0]
    b, n, d = x.shape

    rows = -(-(b + 1) // SUBLANES) * SUBLANES
    cc = jnp.concatenate([c, c_ctx[None, :], jnp.zeros((rows - b - 1, d), F32)], axis=0)
    mod = _ada_call(cc, w_ada, b_ada)

    cos_t, sin_t = _rope_tables(n)
    e_f, e_b = _head_expanders()
    h_zero = jnp.zeros((b, 2, SSM_GROUPS, SSM_STATE, GROUP_W), F32)
    final_g = final_norm_g.reshape(1, d)

    xc = ctx
    for li in range(depth):
        last = li == depth - 1
        lam_init = 0.8 - 0.6 * math.exp(-0.3 * li)
        shift, scale, gate = (mod[li, :b, i * d:(i + 1) * d].reshape(b, 1, d) for i in range(3))
        shift_c, scale_c, gate_c = (mod[li, b:b + 1, i * d:(i + 1) * d].reshape(1, 1, d) for i in range(3))
        w_pack = _pack_w_in(w_in[li])
        w_o = w_out[li].astype(BF16)
        w_a, w_b, w_c = w_o[:D_CHUNK], w_o[D_CHUNK:D_CHUNK + D_SSM], w_o[D_CHUNK + D_SSM:]
        cng = chunk_norm_g[li].reshape(1, D_CHUNK)
        ws_b = chunk_ws[li].astype(BF16)
        bs_exp = jnp.repeat(chunk_bs[li].T, D_CHUNK // A_GROUPS, axis=1)
        conv_w = ssm_conv_w[li]
        conv_b = ssm_conv_b[li].reshape(1, N_CONV_CH)
        dt_bias_p = _pad_lanes(ssm_dt_bias[li])
        a_log_p = _pad_lanes(ssm_a_log[li])
        dskip_e = jnp.repeat(ssm_d[li], SSM_HEAD_DIM).reshape(1, D_SSM)
        norm_g = ssm_norm_g[li].reshape(1, D_SSM)
        lam_p = diff_lambda_p[li]
        subln_g = diff_subln_g[li].reshape(1, ATT_V_DIM)

        c_out = _in_call(xc, scale_c, shift_c, w_pack, cos_t, sin_t, cng, ws_b, bs_exp, rope=False, ctx_only=last)
        xbc_c, dt_c, k_c, v_c = c_out[:4]
        xs_c, bc_c = _conv_call(xbc_c, conv_w, conv_b)
        yf_c, yb_c, h_c = _ssd_call(xs_c, bc_c, dt_c, dt_bias_p, a_log_p, dskip_e, e_f, e_b, h_zero)
        if not last:
            ya_c, z_c, q_c, gc_c = c_out[4:]
            yc_c = _attn_call(q_c, None, None, k_c, v_c, gc_c, lam_p, subln_g, lam_init=lam_init)
            xc_new = _out_call(ya_c, yf_c, yb_c, z_c, yc_c, xc, gate_c, norm_g, w_a, w_b, w_c, final_g,
                               final_norm=False)

        xbc, dt_raw, k, v, ya, z, q, gc = _in_call(x, scale, shift, w_pack, cos_t, sin_t, cng, ws_b, bs_exp,
                                                   rope=True, ctx_only=False)
        xs, bc = _conv_call(xbc, conv_w, conv_b)
        yf, yb, _ = _ssd_call(xs, bc, dt_raw, dt_bias_p, a_log_p, dskip_e, e_f, e_b, h_c)
        yc = _attn_call(q, k, v, k_c, v_c, gc, lam_p, subln_g, lam_init=lam_init)
        x = _out_call(ya, yf, yb, z, yc, x, gate, norm_g, w_a, w_b, w_c, final_g, final_norm=last)
        if not last:
            xc = xc_new
    return x
```

```python
import functools
import math

import jax
import jax.numpy as jnp
from jax import lax
from jax.experimental import pallas as pl
from jax.experimental.pallas import tpu as pltpu

F32 = jnp.float32
BF16 = jnp.bfloat16

EPS = 1e-6
LANES = 128
SUBLANES = 8
V7X_VMEM_BYTES = 64 * 1024 * 1024
VMEM_LIMIT = V7X_VMEM_BYTES * 3 // 4

D_MODEL = 1024
D_CHUNK = 512
CHUNK = 128
A_GROUPS = 4
D_SSM = 1024
SSM_HEADS = 16
SSM_HEAD_DIM = 64
SSM_GROUPS = 2
SSM_STATE = 128
SSM_CONV = 5
SSM_CHUNK = 128
GROUP_W = D_SSM // SSM_GROUPS
HEADS_PER_GROUP = SSM_HEADS // SSM_GROUPS
D_ATT = 512
ATT_QK_DIM = 64
ATT_V_DIM = 128
ATT_HEADS = 4
D_QK = 512
ROPE_THETA = 10000.0
GRID_W = 64
N_CONV_CH = D_SSM + 2 * SSM_GROUPS * SSM_STATE
N_DT = 2 * SSM_HEADS
D_MIX = D_CHUNK + D_SSM + D_ATT

_PACK_SIZES = (("xbc", N_CONV_CH), ("dt", LANES), ("k", D_QK), ("v", D_ATT), ("u", D_CHUNK), ("vm", D_CHUNK),
               ("ga", D_CHUNK), ("z", D_SSM), ("q", D_QK), ("gc", D_ATT))
_PACK = {}
_off = 0
for _name, _size in _PACK_SIZES:
    _PACK[_name] = (_off, _off + _size)
    _off += _size
PACK_COLS = _off
PACK_CTX_COLS = _PACK["v"][1]

Q_SCALE = ATT_QK_DIM ** -0.5 * math.log2(math.e)

TM_IN = 256
TM_CONV = 512
SSD_CHUNKS_PER_STEP = 4
TM_OUT = 512
TQ = 512
TK = 512


def _cparams(semantics):
    return pltpu.CompilerParams(dimension_semantics=semantics, vmem_limit_bytes=VMEM_LIMIT)


def _silu(t):
    return t * jax.nn.sigmoid(t)


def _split3(a):
    hi = a.astype(BF16)
    r = a - hi.astype(F32)
    mid = r.astype(BF16)
    lo = (r - mid.astype(F32)).astype(BF16)
    return hi, mid, lo


def _dot(a, b):
    return jnp.dot(a, b, preferred_element_type=F32)


def _dot_nt(a, b):
    return lax.dot_general(a, b, (((1,), (1,)), ((), ())), preferred_element_type=F32)


def _ada_kernel(c_ref, w_ref, b_ref, o_ref):
    s = _silu(c_ref[...])
    w = w_ref[...]
    s_hi = s.astype(BF16)
    s_lo = (s - s_hi.astype(F32)).astype(BF16)
    w_hi = w.astype(BF16)
    w_lo = (w - w_hi.astype(F32)).astype(BF16)
    o_ref[...] = _dot(s_hi, w_hi) + (_dot(s_hi, w_lo) + _dot(s_lo, w_hi)) + b_ref[...]


def _ada_call(cc, w_ada, b_ada):
    depth, d, d3 = w_ada.shape
    rows = cc.shape[0]
    tn = 512
    return pl.pallas_call(
        _ada_kernel,
        out_shape=jax.ShapeDtypeStruct((depth, rows, d3), F32),
        grid=(depth, d3 // tn),
        in_specs=[pl.BlockSpec((rows, d), lambda l, j: (0, 0)),
                  pl.BlockSpec((None, d, tn), lambda l, j: (l, 0, j)),
                  pl.BlockSpec((None, 1, tn), lambda l, j: (l, 0, j))],
        out_specs=pl.BlockSpec((None, rows, tn), lambda l, j: (l, 0, j)),
        compiler_params=_cparams(("parallel", "parallel")),
        name="ada_mod",
    )(cc, w_ada, b_ada.reshape(depth, 1, d3))


def _rope(t, cos_t, sin_t, lower):
    outs = []
    for j in range(t.shape[1] // LANES):
        tj = t[:, j * LANES:(j + 1) * LANES]
        partner = jnp.where(lower, pltpu.roll(tj, LANES - 32, 1), pltpu.roll(tj, 32, 1))
        outs.append(tj * cos_t + partner * sin_t)
    return jnp.concatenate(outs, axis=1)


def _in_kernel(x_ref, scale_ref, shift_ref, w_ref, cos_ref, sin_ref, cng_ref, ws_ref, bs_ref, *outs,
               rope, ctx_only, tm):
    x = x_ref[...]
    ms = jnp.mean(x * x, axis=-1, keepdims=True)
    h = (x * lax.rsqrt(ms + EPS)) * (1.0 + scale_ref[...]) + shift_ref[...]
    hb = h.astype(BF16)

    def proj(name):
        lo, hi = _PACK[name]
        return _dot(hb, w_ref[:, lo:hi])

    if ctx_only:
        xbc_ref, dt_ref, k_ref, v_ref = outs
    else:
        xbc_ref, dt_ref, k_ref, v_ref, ya_ref, z_ref, q_ref, gc_ref = outs

    xbc_ref[...] = proj("xbc")
    dt_ref[...] = proj("dt")
    v_ref[...] = proj("v").astype(BF16)
    k = proj("k")
    if rope:
        lane = lax.broadcasted_iota(jnp.int32, (tm, LANES), 1)
        lower = (lane & 32) == 0
        cos_t = cos_ref[...]
        sin_t = sin_ref[...]
        k = _rope(k, cos_t, sin_t, lower)
    k_ref[...] = k.astype(BF16)
    if ctx_only:
        return

    q = proj("q")
    if rope:
        q = _rope(q, cos_t, sin_t, lower)
    q_ref[...] = (q * Q_SCALE).astype(BF16)
    z_ref[...] = proj("z")
    gc_ref[...] = proj("gc")

    u = jax.nn.gelu(proj("u"))
    vm = jax.nn.gelu(proj("vm"))
    vn = (vm * lax.rsqrt(jnp.mean(vm * vm, axis=-1, keepdims=True) + EPS)) * cng_ref[...]
    vnb = vn.astype(BF16)
    gate = _silu(proj("ga"))
    for r in range(tm // CHUNK):
        rows = slice(r * CHUNK, (r + 1) * CHUNK)
        for g in range(A_GROUPS):
            cols = slice(g * LANES, (g + 1) * LANES)
            mixed = _dot(ws_ref[g], vnb[rows, cols]) + bs_ref[:, cols]
            ya_ref[rows, cols] = (u[rows, cols] * mixed * gate[rows, cols]).astype(BF16)


def _in_call(x, scale, shift, w_pack, cos_t, sin_t, cng, ws_b, bs_exp, *, rope, ctx_only):
    b, n, d = x.shape
    tm = min(TM_IN, n)
    ncols = PACK_CTX_COLS if ctx_only else PACK_COLS
    per_batch = scale.shape[0] > 1
    mod_map = (lambda i, j: (i, 0, 0)) if per_batch else (lambda i, j: (0, 0, 0))
    tok = lambda width: pl.BlockSpec((None, tm, width), lambda i, j: (i, j, 0))
    const2 = lambda shape: pl.BlockSpec(shape, lambda i, j: (0, 0))
    out_shape = [jax.ShapeDtypeStruct((b, n, N_CONV_CH), F32), jax.ShapeDtypeStruct((b, n, LANES), F32),
                 jax.ShapeDtypeStruct((b, n, D_QK), BF16), jax.ShapeDtypeStruct((b, n, D_ATT), BF16)]
    out_specs = [tok(N_CONV_CH), tok(LANES), tok(D_QK), tok(D_ATT)]
    if not ctx_only:
        out_shape += [jax.ShapeDtypeStruct((b, n, D_CHUNK), BF16), jax.ShapeDtypeStruct((b, n, D_SSM), F32),
                      jax.ShapeDtypeStruct((b, n, D_QK), BF16), jax.ShapeDtypeStruct((b, n, D_ATT), F32)]
        out_specs += [tok(D_CHUNK), tok(D_SSM), tok(D_QK), tok(D_ATT)]
    return pl.pallas_call(
        functools.partial(_in_kernel, rope=rope, ctx_only=ctx_only, tm=tm),
        out_shape=out_shape,
        grid=(b, n // tm),
        in_specs=[tok(d),
                  pl.BlockSpec((None, 1, d), mod_map),
                  pl.BlockSpec((None, 1, d), mod_map),
                  pl.BlockSpec((d, ncols), lambda i, j: (0, 0)),
                  pl.BlockSpec((tm, LANES), lambda i, j: (j, 0)),
                  pl.BlockSpec((tm, LANES), lambda i, j: (j, 0)),
                  const2((1, D_CHUNK)),
                  pl.BlockSpec((A_GROUPS, CHUNK, CHUNK), lambda i, j: (0, 0, 0)),
                  const2((CHUNK, D_CHUNK))],
        out_specs=out_specs,
        compiler_params=_cparams(("parallel", "parallel")),
        name="in_proj_ctx" if ctx_only else "in_proj",
    )(x, scale, shift, w_pack, cos_t, sin_t, cng, ws_b, bs_exp)


HALO = SUBLANES


def _conv_kernel(cur_ref, prev_ref, next_ref, cw_ref, cb_ref, xs_ref, bc_ref, *, nt):
    j = pl.program_id(1)
    tm = cur_ref.shape[0]
    rows = tm + 2 * HALO
    prev = jnp.where(j > 0, prev_ref[...], 0.0)
    nxt = jnp.where(j < nt - 1, next_ref[...], 0.0)
    for c in range(N_CONV_CH // LANES):
        cols = slice(c * LANES, (c + 1) * LANES)
        ext = jnp.concatenate([prev[:, cols], cur_ref[:, cols], nxt[:, cols]], axis=0)
        acc = cb_ref[:, cols] + ext[HALO:HALO + tm] * cw_ref[SSM_CONV // 2:SSM_CONV // 2 + 1, cols]
        for t in range(SSM_CONV):
            if t != SSM_CONV // 2:
                shifted = pltpu.roll(ext, (SSM_CONV // 2 - t) % rows, 0)
                acc = acc + shifted[HALO:HALO + tm] * cw_ref[t:t + 1, cols]
        y = _silu(acc)
        if c < D_SSM // LANES:
            xs_ref[:, cols] = y
        else:
            bc_ref[:, c * LANES - D_SSM:(c + 1) * LANES - D_SSM] = y.astype(BF16)


def _conv_call(xbc, conv_w, conv_b):
    b, n, _ = xbc.shape
    tm = min(TM_CONV, n)
    nt = n // tm
    hpt = tm // HALO
    last_halo = n // HALO - 1
    halo = lambda imap: pl.BlockSpec((None, HALO, N_CONV_CH), imap)
    return pl.pallas_call(
        functools.partial(_conv_kernel, nt=nt),
        out_shape=[jax.ShapeDtypeStruct((b, n, D_SSM), F32),
                   jax.ShapeDtypeStruct((b, n, N_CONV_CH - D_SSM), BF16)],
        grid=(b, nt),
        in_specs=[pl.BlockSpec((None, tm, N_CONV_CH), lambda i, j: (i, j, 0)),
                  halo(lambda i, j: (i, jnp.maximum(j * hpt - 1, 0), 0)),
                  halo(lambda i, j: (i, jnp.minimum((j + 1) * hpt, last_halo), 0)),
                  pl.BlockSpec((SSM_CONV, N_CONV_CH), lambda i, j: (0, 0)),
                  pl.BlockSpec((1, N_CONV_CH), lambda i, j: (0, 0))],
        out_specs=[pl.BlockSpec((None, tm, D_SSM), lambda i, j: (i, j, 0)),
                   pl.BlockSpec((None, tm, N_CONV_CH - D_SSM), lambda i, j: (i, j, 0))],
        compiler_params=_cparams(("parallel", "parallel")),
        name="ssd_conv",
    )(xbc, xbc, xbc, conv_w, conv_b)


def _expand_heads(f, e_mat):
    lane = lax.broadcasted_iota(jnp.int32, f.shape, 1)
    hi, mid, lo = _split3(f)
    packed = jnp.where(lane < 32, hi.astype(F32),
                       jnp.where(lane < 64, pltpu.roll(mid.astype(F32), 32, 1),
                                 jnp.where(lane < 96, pltpu.roll(lo.astype(F32), 64, 1), 0.0)))
    return _dot(packed.astype(BF16), e_mat)


def _ssd_direction(d, x, bc, dt_raw, dtb, aneg, mask, st_ref, e_mat):
    L = SSM_CHUNK
    lane = lax.broadcasted_iota(jnp.int32, (L, LANES), 1)
    t = dt_raw + dtb
    dt = jnp.maximum(t, 0.0) + jnp.log1p(jnp.exp(-jnp.abs(t)))
    a = dt * aneg
    tri = jnp.where(mask, 1.0, 0.0).astype(BF16)
    hi, mid, lo = _split3(a)
    cs = _dot(tri, hi) + _dot(tri, mid) + _dot(tri, lo)
    tot = cs[L - 1:L, :] if d == 0 else cs[0:1, :]
    cs_t = cs.T
    dt_e = _expand_heads(dt, e_mat)
    w_e = _expand_heads(dt * jnp.exp(tot - cs), e_mat)
    ecs_e = _expand_heads(jnp.exp(cs), e_mat)
    sdec = _expand_heads(jnp.broadcast_to(jnp.exp(tot), (SUBLANES, LANES)), e_mat)[0:1, :]
    xdt = (x * dt_e).astype(BF16)
    xw = (x * w_e).astype(BF16)
    ys = []
    for g in range(SSM_GROUPS):
        bm = bc[:, g * SSM_STATE:(g + 1) * SSM_STATE]
        cm = bc[:, (SSM_GROUPS + g) * SSM_STATE:(SSM_GROUPS + g + 1) * SSM_STATE]
        cb = _dot_nt(cm, bm)
        st = st_ref[d, g]
        gcols = slice(g * GROUP_W, (g + 1) * GROUP_W)
        y_off = _dot(cm, st.astype(BF16)) * ecs_e[:, gcols]
        for hp in range(HEADS_PER_GROUP // 2):
            col0 = g * GROUP_W + hp * LANES
            rhs = xdt[:, col0:col0 + LANES]
            pair = []
            for hh in range(2):
                hl = d * SSM_HEADS + g * HEADS_PER_GROUP + 2 * hp + hh
                seg = jnp.where(mask, jnp.exp(cs[:, hl:hl + 1] - cs_t[hl:hl + 1, :]), 0.0)
                pair.append(_dot((cb * seg).astype(BF16), rhs))
            ys.append(jnp.where(lane < SSM_HEAD_DIM, pair[0], pair[1]) + y_off[:, hp * LANES:(hp + 1) * LANES])
        st_ref[d, g] = st * sdec[:, gcols] + _dot(bm.astype(F32).T.astype(BF16), xw[:, gcols])
    return jnp.concatenate(ys, axis=1)


def _ssd_kernel(xf_ref, bcf_ref, dtf_ref, xb_ref, bcb_ref, dtb_ref, dtbias_ref, alog_ref, dskip_ref, ef_ref, eb_ref,
                h0_ref, yf_ref, yb_ref, st_ref, *, nsub):
    @pl.when(pl.program_id(1) == 0)
    def _():
        st_ref[...] = h0_ref[...]

    row = lax.broadcasted_iota(jnp.int32, (SSM_CHUNK, SSM_CHUNK), 0)
    col = lax.broadcasted_iota(jnp.int32, (SSM_CHUNK, SSM_CHUNK), 1)
    aneg = -jnp.exp(alog_ref[...])
    dtbias = dtbias_ref[...]

    for sub in range(nsub):
        fr = slice(sub * SSM_CHUNK, (sub + 1) * SSM_CHUNK)
        br = slice((nsub - 1 - sub) * SSM_CHUNK, (nsub - sub) * SSM_CHUNK)
        x_f = xf_ref[fr, :]
        y_f = _ssd_direction(0, x_f, bcf_ref[fr, :], dtf_ref[fr, :], dtbias, aneg, col <= row, st_ref, ef_ref[...])
        yf_ref[fr, :] = y_f + dskip_ref[...] * x_f
        yb_ref[br, :] = _ssd_direction(1, xb_ref[br, :], bcb_ref[br, :], dtb_ref[br, :], dtbias, aneg, col >= row,
                                       st_ref, eb_ref[...])


def _ssd_call(xs, bc, dt_raw, dt_bias_p, a_log_p, dskip_e, e_f, e_b, h0):
    b, n, _ = xs.shape
    nsub = SSD_CHUNKS_PER_STEP if n % (SSD_CHUNKS_PER_STEP * SSM_CHUNK) == 0 else 1
    rows = nsub * SSM_CHUNK
    nc = n // rows
    fwd = lambda i, s: (i, s, 0)
    bwd = lambda i, s: (i, nc - 1 - s, 0)
    chunk = lambda width, imap: pl.BlockSpec((None, rows, width), imap)
    const2 = lambda shape: pl.BlockSpec(shape, lambda i, s: (0, 0))
    state_spec = pl.BlockSpec((None, 2, SSM_GROUPS, SSM_STATE, GROUP_W), lambda i, s: (i, 0, 0, 0, 0))
    bc_w = N_CONV_CH - D_SSM
    return pl.pallas_call(
        functools.partial(_ssd_kernel, nsub=nsub),
        out_shape=[jax.ShapeDtypeStruct((b, n, D_SSM), F32), jax.ShapeDtypeStruct((b, n, D_SSM), F32),
                   jax.ShapeDtypeStruct((b, 2, SSM_GROUPS, SSM_STATE, GROUP_W), F32)],
        grid=(b, nc),
        in_specs=[chunk(D_SSM, fwd), chunk(bc_w, fwd), chunk(LANES, fwd),
                  chunk(D_SSM, bwd), chunk(bc_w, bwd), chunk(LANES, bwd),
                  const2((1, LANES)), const2((1, LANES)), const2((1, D_SSM)),
                  const2((LANES, D_SSM)), const2((LANES, D_SSM)),
                  state_spec],
        out_specs=[chunk(D_SSM, fwd), chunk(D_SSM, bwd), state_spec],
        compiler_params=_cparams(("parallel", "arbitrary")),
        name="ssd_bidir",
    )(xs, bc, dt_raw, xs, bc, dt_raw, dt_bias_p, a_log_p, dskip_e, e_f, e_b, h0)


def _attn_kernel(*refs, with_latent, n_lat, lam_init):
    if with_latent:
        q_ref, k_ref, v_ref, kc_ref, vc_ref, gc_ref, lamp_ref, sg_ref, o_ref = refs
    else:
        q_ref, kc_ref, vc_ref, gc_ref, lamp_ref, sg_ref, o_ref = refs
    tq = q_ref.shape[0]
    q = q_ref[...].astype(F32)
    lane = lax.broadcasted_iota(jnp.int32, (tq, LANES), 1)
    qz = jnp.concatenate([jnp.where(lane < ATT_QK_DIM, q, 0.0), jnp.where(lane >= ATT_QK_DIM, q, 0.0)],
                         axis=0).astype(BF16)

    tiles = []
    if with_latent:
        tiles += [(k_ref, v_ref, j * TK, TK) for j in range(n_lat // TK)]
    tiles.append((kc_ref, vc_ref, 0, kc_ref.shape[0]))

    def scores(tile):
        kr, _, start, size = tile
        return _dot_nt(qz, kr[start:start + size, :])

    m_run = jnp.full((2 * tq, LANES), -jnp.inf, F32)
    l_run = jnp.zeros((2 * tq, LANES), F32)
    acc = jnp.zeros((2 * tq, LANES), F32)
    s_next = scores(tiles[0])
    for j, tile in enumerate(tiles):
        s = s_next
        if j + 1 < len(tiles):
            s_next = scores(tiles[j + 1])
        _, vr, start, size = tile
        parts = [s[:, i * LANES:(i + 1) * LANES] for i in range(size // LANES)]
        smax = parts[0]
        for pj in parts[1:]:
            smax = jnp.maximum(smax, pj)
        m_new = jnp.maximum(m_run, jnp.max(smax, axis=-1, keepdims=True))
        alpha = jnp.exp2(m_run - m_new)
        ps = [jnp.exp2(pj - m_new) for pj in parts]
        psum = ps[0]
        for pj in ps[1:]:
            psum = psum + pj
        l_run = alpha * l_run + psum
        p = jnp.concatenate(ps, axis=1).astype(BF16)
        acc = alpha * acc + _dot(p, vr[start:start + size, :])
        m_run = m_new

    lp = lamp_ref[...]
    lam = (jnp.exp(jnp.sum(lp[0:1] * lp[1:2], axis=-1, keepdims=True))
           - jnp.exp(jnp.sum(lp[2:3] * lp[3:4], axis=-1, keepdims=True)) + lam_init)
    o_maps = acc / jnp.sum(l_run, axis=-1, keepdims=True)
    o = o_maps[:tq] - lam * o_maps[tq:]
    o = (o * lax.rsqrt(jnp.mean(o * o, axis=-1, keepdims=True) + EPS)) * sg_ref[...] * (1.0 - lam_init)
    o_ref[...] = (o * _silu(gc_ref[...])).astype(BF16)


def _attn_call(q, k, v, kc, vc, gc, lam_p, subln_g, *, lam_init):
    b, n, _ = q.shape
    m = kc.shape[1]
    with_latent = k is not None
    tq = min(TQ, n)
    qmap = lambda i, h, j: (i, j, h)
    slab = lambda rows: pl.BlockSpec((None, rows, LANES), lambda i, h, j: (i, 0, h))
    in_specs = [pl.BlockSpec((None, tq, LANES), qmap)]
    args = [q]
    if with_latent:
        in_specs += [slab(n), slab(n)]
        args += [k, v]
    in_specs += [slab(m), slab(m), pl.BlockSpec((None, tq, LANES), qmap),
                 pl.BlockSpec((4, ATT_QK_DIM), lambda i, h, j: (0, 0)),
                 pl.BlockSpec((1, ATT_V_DIM), lambda i, h, j: (0, 0))]
    args += [kc, vc, gc, lam_p, subln_g]
    return pl.pallas_call(
        functools.partial(_attn_kernel, with_latent=with_latent, n_lat=n, lam_init=lam_init),
        out_shape=jax.ShapeDtypeStruct((b, n, D_ATT), BF16),
        grid=(b, ATT_HEADS, n // tq),
        in_specs=in_specs,
        out_specs=pl.BlockSpec((None, tq, LANES), qmap),
        compiler_params=_cparams(("parallel", "parallel", "arbitrary")),
        name="diff_attn" if with_latent else "diff_attn_ctx",
    )(*args)


def _out_kernel(ya_ref, yf_ref, yb_ref, z_ref, yc_ref, x_ref, gate_ref, ng_ref, wa_ref, wb_ref, wc_ref, fg_ref,
                o_ref, *, final_norm):
    y = (yf_ref[...] + yb_ref[...]) * _silu(z_ref[...])
    parts = []
    for g in range(SSM_GROUPS):
        yg = y[:, g * GROUP_W:(g + 1) * GROUP_W]
        parts.append(yg * lax.rsqrt(jnp.mean(yg * yg, axis=-1, keepdims=True) + EPS))
    yn = (jnp.concatenate(parts, axis=1) * ng_ref[...]).astype(BF16)
    mix = _dot(ya_ref[...], wa_ref[...]) + _dot(yn, wb_ref[...]) + _dot(yc_ref[...], wc_ref[...])
    xo = x_ref[...] + gate_ref[...] * mix
    if final_norm:
        xo = (xo * lax.rsqrt(jnp.mean(xo * xo, axis=-1, keepdims=True) + EPS)) * fg_ref[...]
    o_ref[...] = xo


def _out_call(ya, yf, yb, z, yc, x, gate, norm_g, w_a, w_b, w_c, final_g, *, final_norm):
    b, n, d = x.shape
    tm = min(TM_OUT, n)
    per_batch = gate.shape[0] > 1
    mod_map = (lambda i, j: (i, 0, 0)) if per_batch else (lambda i, j: (0, 0, 0))
    tok = lambda width: pl.BlockSpec((None, tm, width), lambda i, j: (i, j, 0))
    const2 = lambda shape: pl.BlockSpec(shape, lambda i, j: (0, 0))
    return pl.pallas_call(
        functools.partial(_out_kernel, final_norm=final_norm),
        out_shape=jax.ShapeDtypeStruct((b, n, d), F32),
        grid=(b, n // tm),
        in_specs=[tok(D_CHUNK), tok(D_SSM), tok(D_SSM), tok(D_SSM), tok(D_ATT), tok(d),
                  pl.BlockSpec((None, 1, d), mod_map),
                  const2((1, D_SSM)), const2((D_CHUNK, d)), const2((D_SSM, d)), const2((D_ATT, d)),
                  const2((1, d))],
        out_specs=tok(d),
        compiler_params=_cparams(("parallel", "parallel")),
        name="out_proj",
    )(ya, yf, yb, z, yc, x, gate, norm_g, w_a, w_b, w_c, final_g)


def _rope_tables(n):
    rows = n // GRID_W
    row = jnp.repeat(jnp.arange(rows), GRID_W).astype(F32)
    col = jnp.tile(jnp.arange(GRID_W), rows).astype(F32)
    n_freq = ATT_QK_DIM // 4
    inv = ROPE_THETA ** (-jnp.arange(n_freq, dtype=F32) / n_freq)
    ang = jnp.concatenate([row[:, None] * inv, col[:, None] * inv], axis=-1)
    cos, sin = jnp.cos(ang), jnp.sin(ang)
    cos_t = jnp.tile(cos, (1, LANES // (ATT_QK_DIM // 2)))
    sin_t = jnp.tile(jnp.concatenate([-sin, sin], axis=-1), (1, LANES // ATT_QK_DIM))
    return cos_t, sin_t


def _head_expanders():
    k = jnp.arange(LANES)[:, None]
    head = jnp.arange(D_SSM)[None, :] // SSM_HEAD_DIM
    used = k < 3 * N_DT
    e_f = (used & ((k % N_DT) == head)).astype(BF16)
    e_b = (used & ((k % N_DT) == head + SSM_HEADS)).astype(BF16)
    return e_f, e_b


def _pack_w_in(w):
    dt0, dt1 = N_CONV_CH, N_CONV_CH + N_DT
    pad = jnp.zeros((w.shape[0], LANES - N_DT), w.dtype)
    return jnp.concatenate([w[:, :dt0], w[:, dt0:dt1], pad, w[:, dt1:]], axis=1).astype(BF16)


def _pad_lanes(v):
    flat = v.reshape(1, -1)
    return jnp.pad(flat, ((0, 0), (0, LANES - flat.shape[1])))


def kernel(x, c, ctx, c_ctx, w_ada, b_ada, w_in, w_out, chunk_norm_g, chunk_ws, chunk_bs, ssm_conv_w, ssm_conv_b,
           ssm_dt_bias, ssm_a_log, ssm_d, ssm_norm_g, diff_lambda_p, diff_subln_g, final_norm_g):
    depth = w_in.shape[0]
    b, n, d = x.shape

    rows = -(-(b + 1) // SUBLANES) * SUBLANES
    cc = jnp.concatenate([c, c_ctx[None, :], jnp.zeros((rows - b - 1, d), F32)], axis=0)
    mod = _ada_call(cc, w_ada, b_ada)

    cos_t, sin_t = _rope_tables(n)
    e_f, e_b = _head_expanders()
    h_zero = jnp.zeros((b, 2, SSM_GROUPS, SSM_STATE, GROUP_W), F32)
    final_g = final_norm_g.reshape(1, d)

    xc = ctx
    for li in range(depth):
        last = li == depth - 1
        lam_init = 0.8 - 0.6 * math.exp(-0.3 * li)
        shift, scale, gate = (mod[li, :b, i * d:(i + 1) * d].reshape(b, 1, d) for i in range(3))
        shift_c, scale_c, gate_c = (mod[li, b:b + 1, i * d:(i + 1) * d].reshape(1, 1, d) for i in range(3))
        w_pack = _pack_w_in(w_in[li])
        w_o = w_out[li].astype(BF16)
        w_a, w_b, w_c = w_o[:D_CHUNK], w_o[D_CHUNK:D_CHUNK + D_SSM], w_o[D_CHUNK + D_SSM:]
        cng = chunk_norm_g[li].reshape(1, D_CHUNK)
        ws_b = chunk_ws[li].astype(BF16)
        bs_exp = jnp.repeat(chunk_bs[li].T, D_CHUNK // A_GROUPS, axis=1)
        conv_w = ssm_conv_w[li]
        conv_b = ssm_conv_b[li].reshape(1, N_CONV_CH)
        dt_bias_p = _pad_lanes(ssm_dt_bias[li])
        a_log_p = _pad_lanes(ssm_a_log[li])
        dskip_e = jnp.repeat(ssm_d[li], SSM_HEAD_DIM).reshape(1, D_SSM)
        norm_g = ssm_norm_g[li].reshape(1, D_SSM)
        lam_p = diff_lambda_p[li]
        subln_g = diff_subln_g[li].reshape(1, ATT_V_DIM)

        c_out = _in_call(xc, scale_c, shift_c, w_pack, cos_t, sin_t, cng, ws_b, bs_exp, rope=False, ctx_only=last)
        xbc_c, dt_c, k_c, v_c = c_out[:4]
        xs_c, bc_c = _conv_call(xbc_c, conv_w, conv_b)
        yf_c, yb_c, h_c = _ssd_call(xs_c, bc_c, dt_c, dt_bias_p, a_log_p, dskip_e, e_f, e_b, h_zero)
        if not last:
            ya_c, z_c, q_c, gc_c = c_out[4:]
            yc_c = _attn_call(q_c, None, None, k_c, v_c, gc_c, lam_p, subln_g, lam_init=lam_init)
            xc_new = _out_call(ya_c, yf_c, yb_c, z_c, yc_c, xc, gate_c, norm_g, w_a, w_b, w_c, final_g,
                               final_norm=False)

        xbc, dt_raw, k, v, ya, z, q, gc = _in_call(x, scale, shift, w_pack, cos_t, sin_t, cng, ws_b, bs_exp,
                                                   rope=True, ctx_only=False)
        xs, bc = _conv_call(xbc, conv_w, conv_b)
        yf, yb, _ = _ssd_call(xs, bc, dt_raw, dt_bias_p, a_log_p, dskip_e, e_f, e_b, h_c)
        yc = _attn_call(q, k, v, k_c, v_c, gc, lam_p, subln_g, lam_init=lam_init)
        x = _out_call(ya, yf, yb, z, yc, x, gate, norm_g, w_a, w_b, w_c, final_g, final_norm=last)
        if not last:
            xc = xc_new
    return x
```

```python
import functools
import math

import jax
import jax.numpy as jnp
from jax import lax
from jax.experimental import pallas as pl
from jax.experimental.pallas import tpu as pltpu

F32 = jnp.float32
BF16 = jnp.bfloat16

EPS = 1e-6
LANES = 128
SUBLANES = 8
V7X_VMEM_BYTES = 64 * 1024 * 1024
VMEM_LIMIT = V7X_VMEM_BYTES * 3 // 4

D_MODEL = 1024
D_CHUNK = 512
CHUNK = 128
A_GROUPS = 4
D_SSM = 1024
SSM_HEADS = 16
SSM_HEAD_DIM = 64
SSM_GROUPS = 2
SSM_STATE = 128
SSM_CONV = 5
SSM_CHUNK = 128
GROUP_W = D_SSM // SSM_GROUPS
HEADS_PER_GROUP = SSM_HEADS // SSM_GROUPS
D_ATT = 512
ATT_QK_DIM = 64
ATT_V_DIM = 128
ATT_HEADS = 4
D_QK = 512
ROPE_THETA = 10000.0
GRID_W = 64
N_CONV_CH = D_SSM + 2 * SSM_GROUPS * SSM_STATE
N_DT = 2 * SSM_HEADS
D_MIX = D_CHUNK + D_SSM + D_ATT

_PACK_SIZES = (("xbc", N_CONV_CH), ("dt", LANES), ("k", D_QK), ("v", D_ATT), ("u", D_CHUNK), ("vm", D_CHUNK),
               ("ga", D_CHUNK), ("z", D_SSM), ("q", D_QK), ("gc", D_ATT))
_PACK = {}
_off = 0
for _name, _size in _PACK_SIZES:
    _PACK[_name] = (_off, _off + _size)
    _off += _size
PACK_COLS = _off
PACK_CTX_COLS = _PACK["v"][1]

Q_SCALE = ATT_QK_DIM ** -0.5 * math.log2(math.e)

TM_IN = 512
TM_CONV = 512
SSD_CHUNKS_PER_STEP = 8
TM_OUT = 512
TQ = 512
TK = 512


def _cparams(semantics):
    return pltpu.CompilerParams(dimension_semantics=semantics, vmem_limit_bytes=VMEM_LIMIT)


def _silu(t):
    return t * jax.nn.sigmoid(t)


def _split3(a):
    hi = a.astype(BF16)
    r = a - hi.astype(F32)
    mid = r.astype(BF16)
    lo = (r - mid.astype(F32)).astype(BF16)
    return hi, mid, lo


def _dot(a, b):
    return jnp.dot(a, b, preferred_element_type=F32)


def _dot_nt(a, b):
    return lax.dot_general(a, b, (((1,), (1,)), ((), ())), preferred_element_type=F32)


def _ada_kernel(c_ref, w_ref, b_ref, o_ref):
    s = _silu(c_ref[...])
    w = w_ref[...]
    s_hi = s.astype(BF16)
    s_lo = (s - s_hi.astype(F32)).astype(BF16)
    w_hi = w.astype(BF16)
    w_lo = (w - w_hi.astype(F32)).astype(BF16)
    o_ref[...] = _dot(s_hi, w_hi) + (_dot(s_hi, w_lo) + _dot(s_lo, w_hi)) + b_ref[...]


def _ada_call(cc, w_ada, b_ada):
    depth, d, d3 = w_ada.shape
    rows = cc.shape[0]
    tn = 512
    return pl.pallas_call(
        _ada_kernel,
        out_shape=jax.ShapeDtypeStruct((depth, rows, d3), F32),
        grid=(depth, d3 // tn),
        in_specs=[pl.BlockSpec((rows, d), lambda l, j: (0, 0)),
                  pl.BlockSpec((None, d, tn), lambda l, j: (l, 0, j)),
                  pl.BlockSpec((None, 1, tn), lambda l, j: (l, 0, j))],
        out_specs=pl.BlockSpec((None, rows, tn), lambda l, j: (l, 0, j)),
        compiler_params=_cparams(("parallel", "parallel")),
        name="ada_mod",
    )(cc, w_ada, b_ada.reshape(depth, 1, d3))


def _rope(t, cos_t, sin_t, lower):
    outs = []
    for j in range(t.shape[1] // LANES):
        tj = t[:, j * LANES:(j + 1) * LANES]
        partner = jnp.where(lower, pltpu.roll(tj, LANES - 32, 1), pltpu.roll(tj, 32, 1))
        outs.append(tj * cos_t + partner * sin_t)
    return jnp.concatenate(outs, axis=1)


def _in_kernel(x_ref, scale_ref, shift_ref, w_ref, cos_ref, sin_ref, cng_ref, ws_ref, bs_ref, *outs,
               rope, ctx_only, tm):
    x = x_ref[...]
    ms = jnp.mean(x * x, axis=-1, keepdims=True)
    h = (x * lax.rsqrt(ms + EPS)) * (1.0 + scale_ref[...]) + shift_ref[...]
    hb = h.astype(BF16)

    def proj(name):
        lo, hi = _PACK[name]
        return _dot(hb, w_ref[:, lo:hi])

    if ctx_only:
        xbc_ref, dt_ref, k_ref, v_ref = outs
    else:
        xbc_ref, dt_ref, k_ref, v_ref, ya_ref, z_ref, q_ref, gc_ref = outs

    xbc_ref[...] = proj("xbc")
    dt_ref[...] = proj("dt")
    v_ref[...] = proj("v").astype(BF16)
    k = proj("k")
    if rope:
        lane = lax.broadcasted_iota(jnp.int32, (tm, LANES), 1)
        lower = (lane & 32) == 0
        cos_t = cos_ref[...]
        sin_t = sin_ref[...]
        k = _rope(k, cos_t, sin_t, lower)
    k_ref[...] = k.astype(BF16)
    if ctx_only:
        return

    q = proj("q")
    if rope:
        q = _rope(q, cos_t, sin_t, lower)
    q_ref[...] = (q * Q_SCALE).astype(BF16)
    z_ref[...] = proj("z")
    gc_ref[...] = proj("gc")

    u = jax.nn.gelu(proj("u"))
    vm = jax.nn.gelu(proj("vm"))
    vn = (vm * lax.rsqrt(jnp.mean(vm * vm, axis=-1, keepdims=True) + EPS)) * cng_ref[...]
    vnb = vn.astype(BF16)
    gate = _silu(proj("ga"))
    for r in range(tm // CHUNK):
        rows = slice(r * CHUNK, (r + 1) * CHUNK)
        for g in range(A_GROUPS):
            cols = slice(g * LANES, (g + 1) * LANES)
            mixed = _dot(ws_ref[g], vnb[rows, cols]) + bs_ref[:, cols]
            ya_ref[rows, cols] = (u[rows, cols] * mixed * gate[rows, cols]).astype(BF16)


def _in_call(x, scale, shift, w_pack, cos_t, sin_t, cng, ws_b, bs_exp, *, rope, ctx_only):
    b, n, d = x.shape
    tm = min(TM_IN, n)
    ncols = PACK_CTX_COLS if ctx_only else PACK_COLS
    per_batch = scale.shape[0] > 1
    mod_map = (lambda i, j: (i, 0, 0)) if per_batch else (lambda i, j: (0, 0, 0))
    tok = lambda width: pl.BlockSpec((None, tm, width), lambda i, j: (i, j, 0))
    const2 = lambda shape: pl.BlockSpec(shape, lambda i, j: (0, 0))
    out_shape = [jax.ShapeDtypeStruct((b, n, N_CONV_CH), F32), jax.ShapeDtypeStruct((b, n, LANES), F32),
                 jax.ShapeDtypeStruct((b, n, D_QK), BF16), jax.ShapeDtypeStruct((b, n, D_ATT), BF16)]
    out_specs = [tok(N_CONV_CH), tok(LANES), tok(D_QK), tok(D_ATT)]
    if not ctx_only:
        out_shape += [jax.ShapeDtypeStruct((b, n, D_CHUNK), BF16), jax.ShapeDtypeStruct((b, n, D_SSM), F32),
                      jax.ShapeDtypeStruct((b, n, D_QK), BF16), jax.ShapeDtypeStruct((b, n, D_ATT), F32)]
        out_specs += [tok(D_CHUNK), tok(D_SSM), tok(D_QK), tok(D_ATT)]
    return pl.pallas_call(
        functools.partial(_in_kernel, rope=rope, ctx_only=ctx_only, tm=tm),
        out_shape=out_shape,
        grid=(b, n // tm),
        in_specs=[tok(d),
                  pl.BlockSpec((None, 1, d), mod_map),
                  pl.BlockSpec((None, 1, d), mod_map),
                  pl.BlockSpec((d, ncols), lambda i, j: (0, 0)),
                  pl.BlockSpec((tm, LANES), lambda i, j: (j, 0)),
                  pl.BlockSpec((tm, LANES), lambda i, j: (j, 0)),
                  const2((1, D_CHUNK)),
                  pl.BlockSpec((A_GROUPS, CHUNK, CHUNK), lambda i, j: (0, 0, 0)),
                  const2((CHUNK, D_CHUNK))],
        out_specs=out_specs,
        compiler_params=_cparams(("parallel", "parallel")),
        name="in_proj_ctx" if ctx_only else "in_proj",
    )(x, scale, shift, w_pack, cos_t, sin_t, cng, ws_b, bs_exp)


HALO = SUBLANES


def _conv_kernel(cur_ref, prev_ref, next_ref, cw_ref, cb_ref, xs_ref, bc_ref, *, nt):
    j = pl.program_id(1)
    tm = cur_ref.shape[0]
    rows = tm + 2 * HALO
    prev = jnp.where(j > 0, prev_ref[...], 0.0)
    nxt = jnp.where(j < nt - 1, next_ref[...], 0.0)
    for c in range(N_CONV_CH // LANES):
        cols = slice(c * LANES, (c + 1) * LANES)
        ext = jnp.concatenate([prev[:, cols], cur_ref[:, cols], nxt[:, cols]], axis=0)
        acc = cb_ref[:, cols] + ext[HALO:HALO + tm] * cw_ref[SSM_CONV // 2:SSM_CONV // 2 + 1, cols]
        for t in range(SSM_CONV):
            if t != SSM_CONV // 2:
                shifted = pltpu.roll(ext, (SSM_CONV // 2 - t) % rows, 0)
                acc = acc + shifted[HALO:HALO + tm] * cw_ref[t:t + 1, cols]
        y = _silu(acc)
        if c < D_SSM // LANES:
            xs_ref[:, cols] = y
        else:
            bc_ref[:, c * LANES - D_SSM:(c + 1) * LANES - D_SSM] = y.astype(BF16)


def _conv_call(xbc, conv_w, conv_b):
    b, n, _ = xbc.shape
    tm = min(TM_CONV, n)
    nt = n // tm
    hpt = tm // HALO
    last_halo = n // HALO - 1
    halo = lambda imap: pl.BlockSpec((None, HALO, N_CONV_CH), imap)
    return pl.pallas_call(
        functools.partial(_conv_kernel, nt=nt),
        out_shape=[jax.ShapeDtypeStruct((b, n, D_SSM), F32),
                   jax.ShapeDtypeStruct((b, n, N_CONV_CH - D_SSM), BF16)],
        grid=(b, nt),
        in_specs=[pl.BlockSpec((None, tm, N_CONV_CH), lambda i, j: (i, j, 0)),
                  halo(lambda i, j: (i, jnp.maximum(j * hpt - 1, 0), 0)),
                  halo(lambda i, j: (i, jnp.minimum((j + 1) * hpt, last_halo), 0)),
                  pl.BlockSpec((SSM_CONV, N_CONV_CH), lambda i, j: (0, 0)),
                  pl.BlockSpec((1, N_CONV_CH), lambda i, j: (0, 0))],
        out_specs=[pl.BlockSpec((None, tm, D_SSM), lambda i, j: (i, j, 0)),
                   pl.BlockSpec((None, tm, N_CONV_CH - D_SSM), lambda i, j: (i, j, 0))],
        compiler_params=_cparams(("parallel", "parallel")),
        name="ssd_conv",
    )(xbc, xbc, xbc, conv_w, conv_b)


def _expand_heads(f, e_mat):
    lane = lax.broadcasted_iota(jnp.int32, f.shape, 1)
    hi, mid, lo = _split3(f)
    packed = jnp.where(lane < 32, hi.astype(F32),
                       jnp.where(lane < 64, pltpu.roll(mid.astype(F32), 32, 1),
                                 jnp.where(lane < 96, pltpu.roll(lo.astype(F32), 64, 1), 0.0)))
    return _dot(packed.astype(BF16), e_mat)


def _ssd_direction(d, x, bc, dt_raw, dtb, aneg, mask, st_ref, e_mat):
    L = SSM_CHUNK
    lane = lax.broadcasted_iota(jnp.int32, (L, LANES), 1)
    t = dt_raw + dtb
    dt = jnp.maximum(t, 0.0) + jnp.log1p(jnp.exp(-jnp.abs(t)))
    a = dt * aneg
    tri = jnp.where(mask, 1.0, 0.0).astype(BF16)
    hi, mid, lo = _split3(a)
    cs = _dot(tri, hi) + _dot(tri, mid) + _dot(tri, lo)
    tot = cs[L - 1:L, :] if d == 0 else cs[0:1, :]
    cs_t = cs.T
    dt_e = _expand_heads(dt, e_mat)
    w_e = _expand_heads(dt * jnp.exp(tot - cs), e_mat)
    ecs_e = _expand_heads(jnp.exp(cs), e_mat)
    sdec = _expand_heads(jnp.broadcast_to(jnp.exp(tot), (SUBLANES, LANES)), e_mat)[0:1, :]
    xdt = (x * dt_e).astype(BF16)
    xw = (x * w_e).astype(BF16)
    ys = []
    for g in range(SSM_GROUPS):
        bm = bc[:, g * SSM_STATE:(g + 1) * SSM_STATE]
        cm = bc[:, (SSM_GROUPS + g) * SSM_STATE:(SSM_GROUPS + g + 1) * SSM_STATE]
        cb = _dot_nt(cm, bm)
        st = st_ref[d, g]
        gcols = slice(g * GROUP_W, (g + 1) * GROUP_W)
        y_off = _dot(cm, st.astype(BF16)) * ecs_e[:, gcols]
        for hp in range(HEADS_PER_GROUP // 2):
            col0 = g * GROUP_W + hp * LANES
            rhs = xdt[:, col0:col0 + LANES]
            pair = []
            for hh in range(2):
                hl = d * SSM_HEADS + g * HEADS_PER_GROUP + 2 * hp + hh
                seg = jnp.where(mask, jnp.exp(cs[:, hl:hl + 1] - cs_t[hl:hl + 1, :]), 0.0)
                pair.append(_dot((cb * seg).astype(BF16), rhs))
            ys.append(jnp.where(lane < SSM_HEAD_DIM, pair[0], pair[1]) + y_off[:, hp * LANES:(hp + 1) * LANES])
        st_ref[d, g] = st * sdec[:, gcols] + _dot(bm.astype(F32).T.astype(BF16), xw[:, gcols])
    return jnp.concatenate(ys, axis=1)


def _ssd_kernel(xf_ref, bcf_ref, dtf_ref, xb_ref, bcb_ref, dtb_ref, dtbias_ref, alog_ref, dskip_ref, ef_ref, eb_ref,
                h0_ref, yf_ref, yb_ref, st_ref, *, nsub):
    @pl.when(pl.program_id(1) == 0)
    def _():
        st_ref[...] = h0_ref[...]

    row = lax.broadcasted_iota(jnp.int32, (SSM_CHUNK, SSM_CHUNK), 0)
    col = lax.broadcasted_iota(jnp.int32, (SSM_CHUNK, SSM_CHUNK), 1)
    aneg = -jnp.exp(alog_ref[...])
    dtbias = dtbias_ref[...]

    for sub in range(nsub):
        fr = slice(sub * SSM_CHUNK, (sub + 1) * SSM_CHUNK)
        br = slice((nsub - 1 - sub) * SSM_CHUNK, (nsub - sub) * SSM_CHUNK)
        x_f = xf_ref[fr, :]
        y_f = _ssd_direction(0, x_f, bcf_ref[fr, :], dtf_ref[fr, :], dtbias, aneg, col <= row, st_ref, ef_ref[...])
        yf_ref[fr, :] = y_f + dskip_ref[...] * x_f
        yb_ref[br, :] = _ssd_direction(1, xb_ref[br, :], bcb_ref[br, :], dtb_ref[br, :], dtbias, aneg, col >= row,
                                       st_ref, eb_ref[...])


def _ssd_call(xs, bc, dt_raw, dt_bias_p, a_log_p, dskip_e, e_f, e_b, h0):
    b, n, _ = xs.shape
    nsub = SSD_CHUNKS_PER_STEP if n % (SSD_CHUNKS_PER_STEP * SSM_CHUNK) == 0 else 1
    rows = nsub * SSM_CHUNK
    nc = n // rows
    fwd = lambda i, s: (i, s, 0)
    bwd = lambda i, s: (i, nc - 1 - s, 0)
    chunk = lambda width, imap: pl.BlockSpec((None, rows, width), imap)
    const2 = lambda shape: pl.BlockSpec(shape, lambda i, s: (0, 0))
    state_spec = pl.BlockSpec((None, 2, SSM_GROUPS, SSM_STATE, GROUP_W), lambda i, s: (i, 0, 0, 0, 0))
    bc_w = N_CONV_CH - D_SSM
    return pl.pallas_call(
        functools.partial(_ssd_kernel, nsub=nsub),
        out_shape=[jax.ShapeDtypeStruct((b, n, D_SSM), F32), jax.ShapeDtypeStruct((b, n, D_SSM), F32),
                   jax.ShapeDtypeStruct((b, 2, SSM_GROUPS, SSM_STATE, GROUP_W), F32)],
        grid=(b, nc),
        in_specs=[chunk(D_SSM, fwd), chunk(bc_w, fwd), chunk(LANES, fwd),
                  chunk(D_SSM, bwd), chunk(bc_w, bwd), chunk(LANES, bwd),
                  const2((1, LANES)), const2((1, LANES)), const2((1, D_SSM)),
                  const2((LANES, D_SSM)), const2((LANES, D_SSM)),
                  state_spec],
        out_specs=[chunk(D_SSM, fwd), chunk(D_SSM, bwd), state_spec],
        compiler_params=_cparams(("parallel", "arbitrary")),
        name="ssd_bidir",
    )(xs, bc, dt_raw, xs, bc, dt_raw, dt_bias_p, a_log_p, dskip_e, e_f, e_b, h0)


def _attn_kernel(*refs, with_latent, n_lat, lam_init):
    if with_latent:
        q_ref, k_ref, v_ref, kc_ref, vc_ref, gc_ref, lamp_ref, sg_ref, o_ref = refs
    else:
        q_ref, kc_ref, vc_ref, gc_ref, lamp_ref, sg_ref, o_ref = refs
    tq = q_ref.shape[0]
    q = q_ref[...].astype(F32)
    lane = lax.broadcasted_iota(jnp.int32, (tq, LANES), 1)
    qz = jnp.concatenate([jnp.where(lane < ATT_QK_DIM, q, 0.0), jnp.where(lane >= ATT_QK_DIM, q, 0.0)],
                         axis=0).astype(BF16)

    tiles = []
    if with_latent:
        tiles += [(k_ref, v_ref, j * TK, TK) for j in range(n_lat // TK)]
    tiles.append((kc_ref, vc_ref, 0, kc_ref.shape[0]))

    def scores(tile):
        kr, _, start, size = tile
        return _dot_nt(qz, kr[start:start + size, :])

    m_run = jnp.full((2 * tq, LANES), -jnp.inf, F32)
    l_run = jnp.zeros((2 * tq, LANES), F32)
    acc = jnp.zeros((2 * tq, LANES), F32)
    s_next = scores(tiles[0])
    for j, tile in enumerate(tiles):
        s = s_next
        if j + 1 < len(tiles):
            s_next = scores(tiles[j + 1])
        _, vr, start, size = tile
        parts = [s[:, i * LANES:(i + 1) * LANES] for i in range(size // LANES)]
        smax = parts[0]
        for pj in parts[1:]:
            smax = jnp.maximum(smax, pj)
        m_new = jnp.maximum(m_run, jnp.max(smax, axis=-1, keepdims=True))
        alpha = jnp.exp2(m_run - m_new)
        ps = [jnp.exp2(pj - m_new) for pj in parts]
        psum = ps[0]
        for pj in ps[1:]:
            psum = psum + pj
        l_run = alpha * l_run + psum
        p = jnp.concatenate(ps, axis=1).astype(BF16)
        acc = alpha * acc + _dot(p, vr[start:start + size, :])
        m_run = m_new

    lp = lamp_ref[...]
    lam = (jnp.exp(jnp.sum(lp[0:1] * lp[1:2], axis=-1, keepdims=True))
           - jnp.exp(jnp.sum(lp[2:3] * lp[3:4], axis=-1, keepdims=True)) + lam_init)
    o_maps = acc / jnp.sum(l_run, axis=-1, keepdims=True)
    o = o_maps[:tq] - lam * o_maps[tq:]
    o = (o * lax.rsqrt(jnp.mean(o * o, axis=-1, keepdims=True) + EPS)) * sg_ref[...] * (1.0 - lam_init)
    o_ref[...] = (o * _silu(gc_ref[...])).astype(BF16)


def _attn_call(q, k, v, kc, vc, gc, lam_p, subln_g, *, lam_init):
    b, n, _ = q.shape
    m = kc.shape[1]
    with_latent = k is not None
    tq = min(TQ, n)
    qmap = lambda i, h, j: (i, j, h)
    slab = lambda rows: pl.BlockSpec((None, rows, LANES), lambda i, h, j: (i, 0, h))
    in_specs = [pl.BlockSpec((None, tq, LANES), qmap)]
    args = [q]
    if with_latent:
        in_specs += [slab(n), slab(n)]
        args += [k, v]
    in_specs += [slab(m), slab(m), pl.BlockSpec((None, tq, LANES), qmap),
                 pl.BlockSpec((4, ATT_QK_DIM), lambda i, h, j: (0, 0)),
                 pl.BlockSpec((1, ATT_V_DIM), lambda i, h, j: (0, 0))]
    args += [kc, vc, gc, lam_p, subln_g]
    return pl.pallas_call(
        functools.partial(_attn_kernel, with_latent=with_latent, n_lat=n, lam_init=lam_init),
        out_shape=jax.ShapeDtypeStruct((b, n, D_ATT), BF16),
        grid=(b, ATT_HEADS, n // tq),
        in_specs=in_specs,
        out_specs=pl.BlockSpec((None, tq, LANES), qmap),
        compiler_params=_cparams(("parallel", "parallel", "arbitrary")),
        name="diff_attn" if with_latent else "diff_attn_ctx",
    )(*args)


def _out_kernel(ya_ref, yf_ref, yb_ref, z_ref, yc_ref, x_ref, gate_ref, ng_ref, wa_ref, wb_ref, wc_ref, fg_ref,
                o_ref, *, final_norm):
    y = (yf_ref[...] + yb_ref[...]) * _silu(z_ref[...])
    parts = []
    for g in range(SSM_GROUPS):
        yg = y[:, g * GROUP_W:(g + 1) * GROUP_W]
        parts.append(yg * lax.rsqrt(jnp.mean(yg * yg, axis=-1, keepdims=True) + EPS))
    yn = (jnp.concatenate(parts, axis=1) * ng_ref[...]).astype(BF16)
    mix = _dot(ya_ref[...], wa_ref[...]) + _dot(yn, wb_ref[...]) + _dot(yc_ref[...], wc_ref[...])
    xo = x_ref[...] + gate_ref[...] * mix
    if final_norm:
        xo = (xo * lax.rsqrt(jnp.mean(xo * xo, axis=-1, keepdims=True) + EPS)) * fg_ref[...]
    o_ref[...] = xo


def _out_call(ya, yf, yb, z, yc, x, gate, norm_g, w_a, w_b, w_c, final_g, *, final_norm):
    b, n, d = x.shape
    tm = min(TM_OUT, n)
    per_batch = gate.shape[0] > 1
    mod_map = (lambda i, j: (i, 0, 0)) if per_batch else (lambda i, j: (0, 0, 0))
    tok = lambda width: pl.BlockSpec((None, tm, width), lambda i, j: (i, j, 0))
    const2 = lambda shape: pl.BlockSpec(shape, lambda i, j: (0, 0))
    return pl.pallas_call(
        functools.partial(_out_kernel, final_norm=final_norm),
        out_shape=jax.ShapeDtypeStruct((b, n, d), F32),
        grid=(b, n // tm),
        in_specs=[tok(D_CHUNK), tok(D_SSM), tok(D_SSM), tok(D_SSM), tok(D_ATT), tok(d),
                  pl.BlockSpec((None, 1, d), mod_map),
                  const2((1, D_SSM)), const2((D_CHUNK, d)), const2((D_SSM, d)), const2((D_ATT, d)),
                  const2((1, d))],
        out_specs=tok(d),
        compiler_params=_cparams(("parallel", "parallel")),
        name="out_proj",
    )(ya, yf, yb, z, yc, x, gate, norm_g, w_a, w_b, w_c, final_g)


def _rope_tables(n):
    rows = n // GRID_W
    row = jnp.repeat(jnp.arange(rows), GRID_W).astype(F32)
    col = jnp.tile(jnp.arange(GRID_W), rows).astype(F32)
    n_freq = ATT_QK_DIM // 4
    inv = ROPE_THETA ** (-jnp.arange(n_freq, dtype=F32) / n_freq)
    ang = jnp.concatenate([row[:, None] * inv, col[:, None] * inv], axis=-1)
    cos, sin = jnp.cos(ang), jnp.sin(ang)
    cos_t = jnp.tile(cos, (1, LANES // (ATT_QK_DIM // 2)))
    sin_t = jnp.tile(jnp.concatenate([-sin, sin], axis=-1), (1, LANES // ATT_QK_DIM))
    return cos_t, sin_t


def _head_expanders():
    k = jnp.arange(LANES)[:, None]
    head = jnp.arange(D_SSM)[None, :] // SSM_HEAD_DIM
    used = k < 3 * N_DT
    e_f = (used & ((k % N_DT) == head)).astype(BF16)
    e_b = (used & ((k % N_DT) == head + SSM_HEADS)).astype(BF16)
    return e_f, e_b


def _pack_w_in(w):
    dt0, dt1 = N_CONV_CH, N_CONV_CH + N_DT
    pad = jnp.zeros((w.shape[0], LANES - N_DT), w.dtype)
    return jnp.concatenate([w[:, :dt0], w[:, dt0:dt1], pad, w[:, dt1:]], axis=1).astype(BF16)


def _pad_lanes(v):
    flat = v.reshape(1, -1)
    return jnp.pad(flat, ((0, 0), (0, LANES - flat.shape[1])))


def kernel(x, c, ctx, c_ctx, w_ada, b_ada, w_in, w_out, chunk_norm_g, chunk_ws, chunk_bs, ssm_conv_w, ssm_conv_b,
           ssm_dt_bias, ssm_a_log, ssm_d, ssm_norm_g, diff_lambda_p, diff_subln_g, final_norm_g):
    depth = w_in.shape[0]
    b, n, d = x.shape

    rows = -(-(b + 1) // SUBLANES) * SUBLANES
    cc = jnp.concatenate([c, c_ctx[None, :], jnp.zeros((rows - b - 1, d), F32)], axis=0)
    mod = _ada_call(cc, w_ada, b_ada)

    cos_t, sin_t = _rope_tables(n)
    e_f, e_b = _head_expanders()
    h_zero = jnp.zeros((b, 2, SSM_GROUPS, SSM_STATE, GROUP_W), F32)
    final_g = final_norm_g.reshape(1, d)

    xc = ctx
    for li in range(depth):
        last = li == depth - 1
        lam_init = 0.8 - 0.6 * math.exp(-0.3 * li)
        shift, scale, gate = (mod[li, :b, i * d:(i + 1) * d].reshape(b, 1, d) for i in range(3))
        shift_c, scale_c, gate_c = (mod[li, b:b + 1, i * d:(i + 1) * d].reshape(1, 1, d) for i in range(3))
        w_pack = _pack_w_in(w_in[li])
        w_o = w_out[li].astype(BF16)
        w_a, w_b, w_c = w_o[:D_CHUNK], w_o[D_CHUNK:D_CHUNK + D_SSM], w_o[D_CHUNK + D_SSM:]
        cng = chunk_norm_g[li].reshape(1, D_CHUNK)
        ws_b = chunk_ws[li].astype(BF16)
        bs_exp = jnp.repeat(chunk_bs[li].T, D_CHUNK // A_GROUPS, axis=1)
        conv_w = ssm_conv_w[li]
        conv_b = ssm_conv_b[li].reshape(1, N_CONV_CH)
        dt_bias_p = _pad_lanes(ssm_dt_bias[li])
        a_log_p = _pad_lanes(ssm_a_log[li])
        dskip_e = jnp.repeat(ssm_d[li], SSM_HEAD_DIM).reshape(1, D_SSM)
        norm_g = ssm_norm_g[li].reshape(1, D_SSM)
        lam_p = diff_lambda_p[li]
        subln_g = diff_subln_g[li].reshape(1, ATT_V_DIM)

        c_out = _in_call(xc, scale_c, shift_c, w_pack, cos_t, sin_t, cng, ws_b, bs_exp, rope=False, ctx_only=last)
        xbc_c, dt_c, k_c, v_c = c_out[:4]
        xs_c, bc_c = _conv_call(xbc_c, conv_w, conv_b)
        yf_c, yb_c, h_c = _ssd_call(xs_c, bc_c, dt_c, dt_bias_p, a_log_p, dskip_e, e_f, e_b, h_zero)
        if not last:
            ya_c, z_c, q_c, gc_c = c_out[4:]
            yc_c = _attn_call(q_c, None, None, k_c, v_c, gc_c, lam_p, subln_g, lam_init=lam_init)
            xc_new = _out_call(ya_c, yf_c, yb_c, z_c, yc_c, xc, gate_c, norm_g, w_a, w_b, w_c, final_g,
                               final_norm=False)

        xbc, dt_raw, k, v, ya, z, q, gc = _in_call(x, scale, shift, w_pack, cos_t, sin_t, cng, ws_b, bs_exp,
                                                   rope=True, ctx_only=False)
        xs, bc = _conv_call(xbc, conv_w, conv_b)
        yf, yb, _ = _ssd_call(xs, bc, dt_raw, dt_bias_p, a_log_p, dskip_e, e_f, e_b, h_c)
        yc = _attn_call(q, k, v, k_c, v_c, gc, lam_p, subln_g, lam_init=lam_init)
        x = _out_call(ya, yf, yb, z, yc, x, gate, norm_g, w_a, w_b, w_c, final_g, final_norm=last)
        if not last:
            xc = xc_new
    return x
```
